```python
import numpy as np
import jax
import jax.numpy as jnp
from jax import lax

D_MODEL = 1024
BATCH = 32
SEQ = 2048
DEPTH = 2

CHUNK = 64
D_MIX = D_MODEL
N_GROUPS = 4
GROUP_W = D_MIX // N_GROUPS
CONV_W = 3
S5_CH = GROUP_W
S5_GROUP = 16
S5_NG = S5_CH // S5_GROUP
S5_STATE = 64
S5_DT_MIN = 1e-3
S5_DT_MAX = 1e-1
ATT_HD = 64
ATT_QH = GROUP_W // ATT_HD
ATT_KVH = 2
ATT_REP = ATT_QH // ATT_KVH
ATT_SCALE = ATT_HD ** -0.5
WINDOW = 128
WIN_CHUNKS = WINDOW // CHUNK
HG_HEADS = 4
HG_DK = GROUP_W // HG_HEADS
HG_DV = GROUP_W // HG_HEADS
HG_BLOCK = 16
D_FF = 4 * D_MODEL
EPS = 1e-6
SPLIT_SIZES = (GROUP_W, GROUP_W, GROUP_W,
               S5_CH,
               ATT_QH * ATT_HD, ATT_KVH * ATT_HD, ATT_KVH * ATT_HD,
               HG_HEADS * HG_DK, HG_HEADS * HG_DK,
               HG_HEADS * HG_DV, HG_HEADS * HG_DV)
D_IN = sum(SPLIT_SIZES)

kernel_name = "hybrid_parallel_group_stream_encoder"


def rmsnorm(x, gain):
    xf = x.astype(jnp.float32)
    y = xf * lax.rsqrt(jnp.mean(xf * xf, axis=-1, keepdims=True) + EPS)
    return (y * gain.astype(jnp.float32)).astype(x.dtype)


def short_conv_mixer(h, gate_b, gate_c, conv_w):
    seq = h.shape[1]
    z = gate_c * h
    zp = jnp.pad(z, ((0, 0), (CONV_W - 1, 0), (0, 0)))
    acc = conv_w[0] * zp[:, 0:seq]
    for j in range(1, CONV_W):
        acc = acc + conv_w[j] * zp[:, j:j + seq]
    return gate_b * acc


def _complex_affine_combine(e1, e2):
    a1r, a1i, b1r, b1i = e1
    a2r, a2i, b2r, b2i = e2
    return (a2r * a1r - a2i * a1i,
            a2r * a1i + a2i * a1r,
            a2r * b1r - a2i * b1i + b2r,
            a2r * b1i + a2i * b1r + b2i)


def s5_mixer(u, lam_re, lam_im, b_re, b_im, c_re, c_im, d_skip, log_dt, w_glu):
    bsz, seq, _ = u.shape
    uf = u.astype(jnp.float32)
    ug = uf.reshape(bsz, seq, S5_NG, S5_GROUP)
    lr = jnp.minimum(lam_re.astype(jnp.float32), -1e-4)
    li = lam_im.astype(jnp.float32)
    dt = jnp.exp(log_dt.astype(jnp.float32))[:, None]
    mag = jnp.exp(lr * dt)
    ar = mag * jnp.cos(li * dt)
    ai = mag * jnp.sin(li * dt)
    den = lr * lr + li * li
    zr = ((ar - 1.0) * lr + ai * li) / den
    zi = (ai * lr - (ar - 1.0) * li) / den
    bre = b_re.astype(jnp.float32)
    bim = b_im.astype(jnp.float32)
    bbr = zr[..., None] * bre - zi[..., None] * bim
    bbi = zr[..., None] * bim + zi[..., None] * bre
    xr = jnp.einsum('blgh,gph->blgp', ug, bbr)
    xi = jnp.einsum('blgh,gph->blgp', ug, bbi)
    a_r = jnp.broadcast_to(ar, (1, seq, S5_NG, S5_STATE))
    a_i = jnp.broadcast_to(ai, (1, seq, S5_NG, S5_STATE))
    _, _, sr, si = lax.associative_scan(_complex_affine_combine, (a_r, a_i, xr, xi), axis=1)
    y = (jnp.einsum('blgp,ghp->blgh', sr, c_re.astype(jnp.float32))
         - jnp.einsum('blgp,ghp->blgh', si, c_im.astype(jnp.float32)))
    y = y.reshape(bsz, seq, S5_CH) + d_skip.astype(jnp.float32) * uf
    y = jax.nn.gelu(y)
    val, gate = jnp.split(y @ w_glu.astype(jnp.float32), 2, axis=-1)
    return (val * jax.nn.sigmoid(gate)).astype(u.dtype)


def swa_sink_attention(q, k, v, q_gain, k_gain, sinks):
    bsz, seq, _ = q.shape
    nc = seq // CHUNK
    band = (WIN_CHUNKS + 1) * CHUNK
    q = rmsnorm(q.reshape(bsz, seq, ATT_QH, ATT_HD), q_gain)
    k = rmsnorm(k.reshape(bsz, seq, ATT_KVH, ATT_HD), k_gain)
    v = v.reshape(bsz, seq, ATT_KVH, ATT_HD)
    qc = q.reshape(bsz, nc, CHUNK, ATT_KVH, ATT_REP, ATT_HD)
    pad = ((0, 0), (WIN_CHUNKS * CHUNK, 0), (0, 0), (0, 0))
    kp = jnp.pad(k, pad).reshape(bsz, nc + WIN_CHUNKS, CHUNK, ATT_KVH, ATT_HD)
    vp = jnp.pad(v, pad).reshape(bsz, nc + WIN_CHUNKS, CHUNK, ATT_KVH, ATT_HD)
    kb = jnp.concatenate([kp[:, j:j + nc] for j in range(WIN_CHUNKS + 1)], axis=2)
    vb = jnp.concatenate([vp[:, j:j + nc] for j in range(WIN_CHUNKS + 1)], axis=2)
    chunk_idx = jnp.arange(nc)[:, None] + jnp.arange(WIN_CHUNKS + 1)[None, :] - WIN_CHUNKS
    valid = jnp.repeat(chunk_idx >= 0, CHUNK, axis=1)
    s = jnp.einsum('bnqkrd,bnskd->bnkrqs', qc, kb).astype(jnp.float32) * ATT_SCALE
    s = jnp.where(valid[None, :, None, None, None, :], s, -jnp.inf)
    sink = sinks.astype(jnp.float32).reshape(1, 1, ATT_KVH, ATT_REP, 1, 1)
    m = jnp.maximum(jnp.max(s, axis=-1, keepdims=True), sink)
    p = jnp.exp(s - m)
    p = p / (jnp.sum(p, axis=-1, keepdims=True) + jnp.exp(sink - m))
    o = jnp.einsum('bnkrqs,bnskd->bnqkrd', p.astype(vb.dtype), vb)
    del band
    return o.reshape(bsz, seq, ATT_QH * ATT_HD)


def hgrn2_mixer(q, f, i, g, lower_bound, o_gain):
    bsz, seq, _ = q.shape
    nb = seq // HG_BLOCK
    shp_k = (bsz, nb, HG_BLOCK, HG_HEADS, HG_DK)
    shp_v = (bsz, nb, HG_BLOCK, HG_HEADS, HG_DV)
    qf = q.astype(jnp.float32).reshape(shp_k)
    lb = lower_bound.astype(jnp.float32).reshape(HG_HEADS, HG_DK)
    fg = lb + (1.0 - lb) * jax.nn.sigmoid(f.astype(jnp.float32).reshape(shp_k))
    kk = 1.0 - fg
    vv = i.astype(jnp.float32).reshape(shp_v)
    bcum = jnp.cumsum(jnp.log(fg), axis=2)
    causal = jnp.tril(jnp.ones((HG_BLOCK, HG_BLOCK), dtype=bool))
    decay = jnp.exp(jnp.where(causal[:, :, None, None],
                              bcum[:, :, :, None] - bcum[:, :, None, :], -jnp.inf))
    scores = jnp.einsum('bnthk,bnshk,bntshk->bnhts', qf, kk, decay)
    o_intra = jnp.einsum('bnhts,bnshv->bnthv', scores, vv)
    b_last = bcum[:, :, -1]
    k_end = kk * jnp.exp(b_last[:, :, None] - bcum)
    upd = jnp.einsum('bnshk,bnshv->bnhkv', k_end, vv)

    def step(state, xs):
        dec, u = xs
        return dec[..., None] * state + u, state

    init = jnp.zeros((bsz, HG_HEADS, HG_DK, HG_DV), jnp.float32)
    _, s_prev = lax.scan(step, init, (jnp.moveaxis(jnp.exp(b_last), 1, 0), jnp.moveaxis(upd, 1, 0)))
    s_prev = jnp.moveaxis(s_prev, 0, 1)
    o_inter = jnp.einsum('bnthk,bnhkv->bnthv', qf * jnp.exp(bcum), s_prev)
    o = (o_intra + o_inter).reshape(bsz, seq, HG_HEADS, HG_DV)
    gate = jax.nn.silu(g.astype(jnp.float32).reshape(bsz, seq, HG_HEADS, HG_DV))
    o = rmsnorm(o, o_gain) * gate
    return o.reshape(bsz, seq, HG_HEADS * HG_DV).astype(q.dtype)


def setup_inputs(seed: int = 0) -> dict:
    key = jax.random.key(seed)
    ks = jax.random.split(key, 24)
    f32 = jnp.float32
    nrm = lambda k, shape, scale: jax.random.normal(k, shape, f32) * scale
    lam_im0 = jnp.pi * jnp.arange(S5_STATE, dtype=f32)
    return {
        'x': jax.random.normal(ks[0], (BATCH, SEQ, D_MODEL), f32),
        'w_in': nrm(ks[1], (DEPTH, D_MODEL, D_IN), D_MODEL ** -0.5),
        'w_out': nrm(ks[2], (DEPTH, D_MIX, D_MODEL), D_MIX ** -0.5),
        'norm_mix': 1.0 + nrm(ks[3], (DEPTH, D_MODEL), 0.05),
        'norm_ffn': 1.0 + nrm(ks[4], (DEPTH, D_MODEL), 0.05),
        'conv_w': nrm(ks[5], (DEPTH, CONV_W, GROUP_W), CONV_W ** -0.5),
        's5_lam_re': -0.5 + nrm(ks[6], (DEPTH, S5_NG, S5_STATE), 0.01),
        's5_lam_im': lam_im0 + nrm(ks[7], (DEPTH, S5_NG, S5_STATE), 0.01),
        's5_b_re': nrm(ks[8], (DEPTH, S5_NG, S5_STATE, S5_GROUP), (2 * S5_GROUP) ** -0.5),
        's5_b_im': nrm(ks[9], (DEPTH, S5_NG, S5_STATE, S5_GROUP), (2 * S5_GROUP) ** -0.5),
        's5_c_re': nrm(ks[10], (DEPTH, S5_NG, S5_GROUP, S5_STATE), (2 * S5_STATE) ** -0.5),
        's5_c_im': nrm(ks[11], (DEPTH, S5_NG, S5_GROUP, S5_STATE), (2 * S5_STATE) ** -0.5),
        's5_d': nrm(ks[12], (DEPTH, S5_CH), 1.0),
        's5_log_dt': jax.random.uniform(ks[13], (DEPTH, S5_NG), f32,
                                        np.log(S5_DT_MIN).astype(np.float32),
                                        np.log(S5_DT_MAX).astype(np.float32)),
        's5_w_glu': nrm(ks[14], (DEPTH, S5_CH, 2 * S5_CH), S5_CH ** -0.5),
        'attn_q_norm': 1.0 + nrm(ks[15], (DEPTH, ATT_HD), 0.05),
        'attn_k_norm': 1.0 + nrm(ks[16], (DEPTH, ATT_HD), 0.05),
        'attn_sinks': nrm(ks[17], (DEPTH, ATT_QH), 0.5),
        'hg_lower_bounds': 1.0 + nrm(ks[18], (DEPTH, HG_HEADS * HG_DK), 0.1),
        'hg_out_norm': 1.0 + nrm(ks[19], (DEPTH, HG_DV), 0.05),
        'group_norm': 1.0 + nrm(ks[20], (DEPTH, D_MIX), 0.05),
        'w_ff1': nrm(ks[21], (DEPTH, D_MODEL, D_FF), D_MODEL ** -0.5),
        'w_ff2': nrm(ks[22], (DEPTH, D_FF, D_MODEL), D_FF ** -0.5),
    }


def reference(x, w_in, w_out, norm_mix, norm_ffn, conv_w, s5_lam_re, s5_lam_im,
              s5_b_re, s5_b_im, s5_c_re, s5_c_im, s5_d, s5_log_dt, s5_w_glu,
              attn_q_norm, attn_k_norm, attn_sinks, hg_lower_bounds, hg_out_norm,
              group_norm, w_ff1, w_ff2):
    bsz, seq, _ = x.shape
    split_idx = [int(v) for v in np.cumsum(SPLIT_SIZES)[:-1]]
    lb_p = jax.nn.softmax(hg_lower_bounds.astype(jnp.float32), axis=0)
    lb_table = jnp.cumsum(lb_p, axis=0) - lb_p[0]
    for l in range(DEPTH):
        h = rmsnorm(x, norm_mix[l])
        proj = h @ w_in[l]
        (cv_h, cv_b, cv_c, s5_u, a_q, a_k, a_v,
         hg_q, hg_f, hg_i, hg_g) = jnp.split(proj, split_idx, axis=-1)
        y_a = short_conv_mixer(cv_h, cv_b, cv_c, conv_w[l])
        y_b = s5_mixer(s5_u, s5_lam_re[l], s5_lam_im[l], s5_b_re[l], s5_b_im[l],
                       s5_c_re[l], s5_c_im[l], s5_d[l], s5_log_dt[l], s5_w_glu[l])
        y_c = swa_sink_attention(a_q, a_k, a_v, attn_q_norm[l], attn_k_norm[l], attn_sinks[l])
        y_d = hgrn2_mixer(hg_q, hg_f, hg_i, hg_g, lb_table[l], hg_out_norm[l])
        y = jnp.stack([y_a.astype(x.dtype), y_b.astype(x.dtype),
                       y_c.astype(x.dtype), y_d.astype(x.dtype)], axis=2)
        y = rmsnorm(y, group_norm[l].reshape(N_GROUPS, GROUP_W)).reshape(bsz, seq, D_MIX)
        x = x + (y @ w_out[l]).astype(x.dtype)
        h = rmsnorm(x, norm_ffn[l])
        x = x + (jnp.square(jax.nn.relu(h @ w_ff1[l])) @ w_ff2[l]).astype(x.dtype)
    return x
```

```python
import functools

import jax
import jax.numpy as jnp
from jax import lax
from jax.experimental import pallas as pl
from jax.experimental.pallas import tpu as pltpu

F32 = jnp.float32
BF16 = jnp.bfloat16

D_MODEL = 1024
GROUP_W = 256
N_GROUPS = 4
CHUNK = 64
CONV_W = 3
S5_GROUP = 16
S5_NG = 16
S5_STATE = 64
S5_BLOCK = 16
ATT_HD = 64
ATT_QH = 4
ATT_KVH = 2
ATT_REP = 2
ATT_SCALE = ATT_HD ** -0.5
WIN_CHUNKS = 2
HG_HEADS = 4
HG_DK = 64
HG_BLOCK = 16
D_FF = 4096
D_IN = 2560
EPS = 1e-6

C_CVH, C_CVB, C_CVC, C_S5U = 0, 256, 512, 768
C_AQ, C_AK, C_AV = 1024, 1280, 1408
C_HQ, C_HF, C_HI, C_HG = 1536, 1792, 2048, 2304

SEQ_TILE = 256
FFN_TILE = 512
FFN_COLS = 1024
VMEM_LIMIT_BYTES = 56 * 1024 * 1024


def _dot(a, b):
    return jnp.dot(a, b, preferred_element_type=F32)


def _dot_nt(a, b):
    return lax.dot_general(a, b, (((1,), (1,)), ((), ())), preferred_element_type=F32)


def _dot_tn(a, b):
    return lax.dot_general(a, b, (((0,), (0,)), ((), ())), preferred_element_type=F32)


def _segment_ones(n, seg):
    r = lax.broadcasted_iota(jnp.int32, (n, n), 0) // seg
    c = lax.broadcasted_iota(jnp.int32, (n, n), 1) // seg
    return (r == c).astype(BF16)


def _chunk_transpose16(arrs):
    rows = arrs[0].shape[0]
    chunk = lax.broadcasted_iota(jnp.int32, (rows, 256), 1) // 16
    for j in range(4):
        s = 1 << j
        low = ((chunk >> j) & 1) == 0
        new = list(arrs)
        for i in range(16):
            if i & s:
                continue
            a, b = arrs[i], arrs[i + s]
            new[i] = jnp.where(low, a, pltpu.roll(b, 16 * s, axis=1))
            new[i + s] = jnp.where(low, pltpu.roll(a, 256 - 16 * s, axis=1), b)
        arrs = new
    return arrs


def _mixer_kernel(sinks_ref, x_ref, nm_ref, win_ref, wout_ref, convw_ref, w1_ref, cexp_ref,
                  pq_ref, dskip_ref, wglu_ref, qg_ref, kg_ref, lb_ref, hgn_ref, gn_ref,
                  o_ref,
                  proj_sc, zp_sc, kext_sc, vext_sc, hgb_sc, hgk_sc, hgv_sc, hgs_sc, hgo_sc,
                  s5u_sc, s5s_sc, s5c_sc, s5w_sc, s5i_sc, s5y_sc, s5o_sc, att_sc, ycat_sc,
                  *, tt):
    t = pl.program_id(1)
    nb = tt // S5_BLOCK
    nhb = tt // HG_BLOCK
    ncp = tt // CHUNK
    halo = HG_BLOCK
    band = (WIN_CHUNKS + 1) * CHUNK
    hist = WIN_CHUNKS * CHUNK

    @pl.when(t == 0)
    def _reset():
        zp_sc[0:8, :] = jnp.zeros((8, GROUP_W), F32)
        kext_sc[0:hist, :] = jnp.zeros((hist, 128), F32)
        vext_sc[0:hist, :] = jnp.zeros((hist, 128), F32)
        hgb_sc[0:halo, :] = jnp.zeros((halo, GROUP_W), F32)
        hgk_sc[0:halo, :] = jnp.zeros((halo, GROUP_W), F32)
        hgv_sc[0:halo, :] = jnp.zeros((halo, GROUP_W), F32)
        hgs_sc[...] = jnp.zeros((GROUP_W, GROUP_W), F32)
        s5s_sc[...] = jnp.zeros((2, S5_NG, 128), F32)

    x = x_ref[0]
    ms = jnp.mean(x * x, axis=-1, keepdims=True)
    hn = (x * lax.rsqrt(ms + EPS) * nm_ref[...]).astype(BF16)
    for c0 in range(0, D_IN, 512):
        proj_sc[:, c0:c0 + 512] = _dot(hn, win_ref[:, c0:c0 + 512])

    def group_norm(y, g):
        gain = gn_ref[:, g * GROUP_W:(g + 1) * GROUP_W]
        msq = jnp.mean(y * y, axis=-1, keepdims=True)
        return (y * lax.rsqrt(msq + EPS) * gain).astype(BF16)

    ones256 = _segment_ones(256, 64)
    ones128 = _segment_ones(128, 64)

    z = proj_sc[:, C_CVC:C_CVC + GROUP_W] * proj_sc[:, C_CVH:C_CVH + GROUP_W]
    zp_sc[8:8 + tt, :] = z
    acc = (convw_ref[0:1, :] * zp_sc[6:6 + tt, :] + convw_ref[1:2, :] * zp_sc[7:7 + tt, :]
           + convw_ref[2:3, :] * z)
    y_a = proj_sc[:, C_CVB:C_CVB + GROUP_W] * acc
    zp_sc[0:8, :] = zp_sc[tt:tt + 8, :]
    ycat_sc[:, 0:GROUP_W] = group_norm(y_a, 0)

    for half in range(2):
        s5u_sc[half] = proj_sc[:, C_S5U + 128 * half:C_S5U + 128 * (half + 1)]
    folded = [jnp.concatenate([s5u_sc[half, pl.ds(k, nb, stride=S5_BLOCK), :] for half in range(2)],
                              axis=1)
              for k in range(S5_BLOCK)]
    per_group = _chunk_transpose16(folded)
    for g in range(S5_NG):
        zg = _dot(per_group[g].astype(BF16), w1_ref[g])
        s5y_sc[g] = zg[:, 0:GROUP_W]
        s5c_sc[pl.ds(g, nb, stride=S5_NG), :] = zg[:, GROUP_W:GROUP_W + 128]
    s5w_sc[...] = pltpu.roll(s5c_sc[...], 64, axis=1)
    p_rot, q_rot, q_swp = pq_ref[0], pq_ref[1], pq_ref[2]
    st, sw = s5s_sc[0], s5s_sc[1]
    for b in range(nb):
        s5i_sc[b * S5_NG:(b + 1) * S5_NG, :] = st
        c_b = s5c_sc[b * S5_NG:(b + 1) * S5_NG, :]
        w_b = s5w_sc[b * S5_NG:(b + 1) * S5_NG, :]
        st, sw = (p_rot * st + q_rot * sw + c_b, p_rot * sw + q_swp * st + w_b)
    s5s_sc[0] = st
    s5s_sc[1] = sw
    outs = []
    for g in range(S5_NG):
        s_in = s5i_sc[pl.ds(g, nb, stride=S5_NG), :]
        outs.append(s5y_sc[g] + _dot(s_in.astype(BF16), cexp_ref[g]))
    unfolded = _chunk_transpose16(outs)
    for k in range(S5_BLOCK):
        for half in range(2):
            s5o_sc[half, pl.ds(k, nb, stride=S5_BLOCK), :] = unfolded[k][:, 128 * half:128 * (half + 1)]
    y5 = (jnp.concatenate([s5o_sc[0], s5o_sc[1]], axis=1)
          + dskip_ref[...] * proj_sc[:, C_S5U:C_S5U + GROUP_W])
    y5 = jax.nn.gelu(y5)
    glu = _dot(y5.astype(BF16), wglu_ref[...])
    y_b = glu[:, 0:GROUP_W] * jax.nn.sigmoid(glu[:, GROUP_W:2 * GROUP_W])
    ycat_sc[:, GROUP_W:2 * GROUP_W] = group_norm(y_b, 1)

    aq = proj_sc[:, C_AQ:C_AQ + 256]
    ak = proj_sc[:, C_AK:C_AK + 128]
    q_ms = _dot((aq * aq).astype(BF16), ones256) * (1.0 / ATT_HD)
    k_ms = _dot((ak * ak).astype(BF16), ones128) * (1.0 / ATT_HD)
    qn = aq * lax.rsqrt(q_ms + EPS) * qg_ref[...] * ATT_SCALE
    kn = ak * lax.rsqrt(k_ms + EPS) * kg_ref[...]
    kext_sc[hist:hist + tt, :] = kn
    vext_sc[hist:hist + tt, :] = proj_sc[:, C_AV:C_AV + 128]
    col = lax.broadcasted_iota(jnp.int32, (ATT_REP * CHUNK, band), 1)
    row = lax.broadcasted_iota(jnp.int32, (ATT_REP * CHUNK, 1), 0)
    for c in range(ncp):
        first_valid = jnp.maximum(WIN_CHUNKS - (t * ncp + c), 0) * CHUNK
        for g in range(ATT_KVH):
            q2 = jnp.concatenate(
                [qn[c * CHUNK:(c + 1) * CHUNK, (ATT_REP * g + r) * ATT_HD:(ATT_REP * g + r + 1) * ATT_HD]
                 for r in range(ATT_REP)], axis=0)
            kb = kext_sc[c * CHUNK:c * CHUNK + band, g * ATT_HD:(g + 1) * ATT_HD]
            vb = vext_sc[c * CHUNK:c * CHUNK + band, g * ATT_HD:(g + 1) * ATT_HD]
            s = _dot_nt(q2.astype(BF16), kb.astype(BF16))
            s = jnp.where(col >= first_valid, s, -jnp.inf)
            sink = jnp.where(row < CHUNK, sinks_ref[ATT_REP * g], sinks_ref[ATT_REP * g + 1])
            m = jnp.maximum(jnp.max(s, axis=-1, keepdims=True), sink)
            p = jnp.exp(s - m)
            den = jnp.sum(p, axis=-1, keepdims=True) + jnp.exp(sink - m)
            o2 = _dot(p.astype(BF16), vb.astype(BF16)) / den
            for r in range(ATT_REP):
                h = ATT_REP * g + r
                att_sc[c * CHUNK:(c + 1) * CHUNK, h * ATT_HD:(h + 1) * ATT_HD] = (
                    o2[r * CHUNK:(r + 1) * CHUNK, :])
    kext_sc[0:hist, :] = kext_sc[tt:tt + hist, :]
    vext_sc[0:hist, :] = vext_sc[tt:tt + hist, :]
    ycat_sc[:, 2 * GROUP_W:3 * GROUP_W] = group_norm(att_sc[...], 2)

    hq = proj_sc[:, C_HQ:C_HQ + GROUP_W]
    lb = lb_ref[...]
    fg = lb + (1.0 - lb) * jax.nn.sigmoid(proj_sc[:, C_HF:C_HF + GROUP_W])
    kk = 1.0 - fg
    vv = proj_sc[:, C_HI:C_HI + GROUP_W]
    pos = lax.broadcasted_iota(jnp.int32, (tt, GROUP_W), 0) % HG_BLOCK
    bc = jnp.log(fg)
    for sh in (1, 2, 4, 8):
        hgb_sc[halo:halo + tt, :] = bc
        bc = bc + jnp.where(pos >= sh, hgb_sc[halo - sh:halo - sh + tt, :], 0.0)
    hgb_sc[halo:halo + tt, :] = bc
    hgk_sc[halo:halo + tt, :] = kk
    hgv_sc[halo:halo + tt, :] = vv
    o_intra = _dot((hq * kk).astype(BF16), ones256) * vv
    for d in range(1, HG_BLOCK):
        b_s = hgb_sc[halo - d:halo - d + tt, :]
        k_s = hgk_sc[halo - d:halo - d + tt, :]
        v_s = hgv_sc[halo - d:halo - d + tt, :]
        e = jnp.where(pos >= d, hq * k_s * jnp.exp(bc - b_s), 0.0)
        o_intra = o_intra + _dot(e.astype(BF16), ones256) * v_s
    hgo_sc[...] = o_intra
    qd = hq * jnp.exp(bc)
    head_r = lax.broadcasted_iota(jnp.int32, (GROUP_W, GROUP_W), 0) // HG_DK
    head_c = lax.broadcasted_iota(jnp.int32, (GROUP_W, GROUP_W), 1) // HG_DK
    same_head = head_r == head_c
    for n in range(nhb):
        r0 = n * HG_BLOCK
        b_n = hgb_sc[halo + r0:halo + r0 + HG_BLOCK, :]
        b_last = hgb_sc[halo + r0 + HG_BLOCK - 1:halo + r0 + HG_BLOCK, :]
        k_end = hgk_sc[halo + r0:halo + r0 + HG_BLOCK, :] * jnp.exp(b_last - b_n)
        v_n = hgv_sc[halo + r0:halo + r0 + HG_BLOCK, :]
        s_t = hgs_sc[...]
        o_inter = _dot_nt(qd[r0:r0 + HG_BLOCK, :].astype(BF16), s_t.astype(BF16))
        hgo_sc[r0:r0 + HG_BLOCK, :] = hgo_sc[r0:r0 + HG_BLOCK, :] + o_inter
        upd_t = _dot_tn(v_n.astype(BF16), k_end.astype(BF16))
        hgs_sc[...] = s_t * jnp.exp(b_last) + jnp.where(same_head, upd_t, 0.0)
    o_h = hgo_sc[...]
    o_ms = _dot((o_h * o_h).astype(BF16), ones256) * (1.0 / HG_DK)
    gate = proj_sc[:, C_HG:C_HG + GROUP_W]
    y_d = o_h * lax.rsqrt(o_ms + EPS) * hgn_ref[...] * (gate * jax.nn.sigmoid(gate))
    ycat_sc[:, 3 * GROUP_W:4 * GROUP_W] = group_norm(y_d, 3)

    o_ref[0] = x + _dot(ycat_sc[...], wout_ref[...])


def _mixer_layer(x, sinks, nm, w_in, w_out, conv_w, w1, cexp, pq, dskip, wglu, qg, kg, lb, hgn, gn):
    bsz, seq, _ = x.shape
    tt = min(SEQ_TILE, seq)
    assert seq % tt == 0 and tt % CHUNK == 0 and tt >= WIN_CHUNKS * CHUNK
    nb = tt // S5_BLOCK
    const2 = lambda b, t: (0, 0)
    const3 = lambda b, t: (0, 0, 0)
    in_specs = [
        pl.BlockSpec(memory_space=pltpu.SMEM),
        pl.BlockSpec((1, tt, D_MODEL), lambda b, t: (b, t, 0)),
        pl.BlockSpec((1, D_MODEL), const2),
        pl.BlockSpec((D_MODEL, D_IN), const2),
        pl.BlockSpec((D_MODEL, D_MODEL), const2),
        pl.BlockSpec((CONV_W, GROUP_W), const2),
        pl.BlockSpec((S5_NG, 256, 384), const3),
        pl.BlockSpec((S5_NG, 128, 256), const3),
        pl.BlockSpec((3, S5_NG, 128), const3),
        pl.BlockSpec((1, GROUP_W), const2),
        pl.BlockSpec((GROUP_W, 2 * GROUP_W), const2),
        pl.BlockSpec((1, 256), const2),
        pl.BlockSpec((1, 128), const2),
        pl.BlockSpec((1, GROUP_W), const2),
        pl.BlockSpec((1, GROUP_W), const2),
        pl.BlockSpec((1, D_MODEL), const2),
    ]
    scratch = [
        pltpu.VMEM((tt, D_IN), F32),
        pltpu.VMEM((8 + tt, GROUP_W), F32),
        pltpu.VMEM((WIN_CHUNKS * CHUNK + tt, 128), F32),
        pltpu.VMEM((WIN_CHUNKS * CHUNK + tt, 128), F32),
        pltpu.VMEM((HG_BLOCK + tt, GROUP_W), F32),
        pltpu.VMEM((HG_BLOCK + tt, GROUP_W), F32),
        pltpu.VMEM((HG_BLOCK + tt, GROUP_W), F32),
        pltpu.VMEM((GROUP_W, GROUP_W), F32),
        pltpu.VMEM((tt, GROUP_W), F32),
        pltpu.VMEM((2, tt, 128), F32),
        pltpu.VMEM((2, S5_NG, 128), F32),
        pltpu.VMEM((nb * S5_NG, 128), F32),
        pltpu.VMEM((nb * S5_NG, 128), F32),
        pltpu.VMEM((nb * S5_NG, 128), F32),
        pltpu.VMEM((S5_NG, nb, GROUP_W), F32),
        pltpu.VMEM((2, tt, 128), F32),
        pltpu.VMEM((tt, GROUP_W), F32),
        pltpu.VMEM((tt, D_MODEL), BF16),
    ]
    return pl.pallas_call(
        functools.partial(_mixer_kernel, tt=tt),
        grid=(bsz, seq // tt),
        in_specs=in_specs,
        out_specs=pl.BlockSpec((1, tt, D_MODEL), lambda b, t: (b, t, 0)),
        out_shape=jax.ShapeDtypeStruct(x.shape, x.dtype),
        scratch_shapes=scratch,
        compiler_params=pltpu.CompilerParams(
            dimension_semantics=("parallel", "arbitrary"),
            vmem_limit_bytes=VMEM_LIMIT_BYTES),
        name="mixer_layer",
    )(sinks, x, nm, w_in, w_out, conv_w, w1, cexp, pq, dskip, wglu, qg, kg, lb, hgn, gn)


def _ffn_kernel(x_ref, nf_ref, w1_ref, w2_ref, o_ref, acc_sc):
    x = x_ref[...]
    ms = jnp.mean(x * x, axis=-1, keepdims=True)
    hn = (x * lax.rsqrt(ms + EPS) * nf_ref[...]).astype(BF16)
    for j in range(D_FF // FFN_COLS):
        a = jnp.maximum(_dot(hn, w1_ref[:, j * FFN_COLS:(j + 1) * FFN_COLS]), 0.0)
        part = _dot((a * a).astype(BF16), w2_ref[j * FFN_COLS:(j + 1) * FFN_COLS, :])
        if j == 0:
            acc_sc[...] = x + part
        else:
            acc_sc[...] += part
    o_ref[...] = acc_sc[...]


def _ffn_layer(x2, nf, w1, w2):
    rows = x2.shape[0]
    tm = min(FFN_TILE, rows)
    assert rows % tm == 0
    return pl.pallas_call(
        _ffn_kernel,
        grid=(rows // tm,),
        in_specs=[
            pl.BlockSpec((tm, D_MODEL), lambda i: (i, 0)),
            pl.BlockSpec((1, D_MODEL), lambda i: (0, 0)),
            pl.BlockSpec((D_MODEL, D_FF), lambda i: (0, 0)),
            pl.BlockSpec((D_FF, D_MODEL), lambda i: (0, 0)),
        ],
        out_specs=pl.BlockSpec((tm, D_MODEL), lambda i: (i, 0)),
        out_shape=jax.ShapeDtypeStruct(x2.shape, x2.dtype),
        scratch_shapes=[pltpu.VMEM((tm, D_MODEL), F32)],
        compiler_params=pltpu.CompilerParams(
            dimension_semantics=("parallel",),
            vmem_limit_bytes=VMEM_LIMIT_BYTES),
        name="ffn_layer",
    )(x2, nf, w1, w2)


def _s5_tables(lam_re, lam_im, b_re, b_im, c_re, c_im, log_dt):
    hp = lax.Precision.HIGHEST
    lr = jnp.minimum(lam_re.astype(F32), -1e-4)
    li = lam_im.astype(F32)
    dt = jnp.exp(log_dt.astype(F32))[:, None]
    mag = jnp.exp(lr * dt)
    ar = mag * jnp.cos(li * dt)
    ai = mag * jnp.sin(li * dt)
    den = lr * lr + li * li
    zr = ((ar - 1.0) * lr + ai * li) / den
    zi = (ai * lr - (ar - 1.0) * li) / den
    bre = b_re.astype(F32)
    bim = b_im.astype(F32)
    bbr = zr[..., None] * bre - zi[..., None] * bim
    bbi = zr[..., None] * bim + zi[..., None] * bre
    d = jnp.arange(S5_BLOCK + 1, dtype=F32)[:, None, None]
    magd = jnp.exp(d * (lr * dt))
    pr = magd * jnp.cos(d * (li * dt))
    pi = magd * jnp.sin(d * (li * dt))
    cre = c_re.astype(F32)[None]
    cim = c_im.astype(F32)[None]
    car = cre * pr[:, :, None, :] - cim * pi[:, :, None, :]
    cai = cre * pi[:, :, None, :] + cim * pr[:, :, None, :]
    kern = (jnp.einsum('dghp,gpj->dghj', car[:S5_BLOCK], bbr, precision=hp)
            - jnp.einsum('dghp,gpj->dghj', cai[:S5_BLOCK], bbi, precision=hp))
    idx = jnp.arange(S5_BLOCK)
    lag = idx[None, :] - idx[:, None]
    toep = jnp.where((lag >= 0)[:, :, None, None, None], kern[jnp.clip(lag, 0, S5_BLOCK - 1)], 0.0)
    toep = toep.transpose(2, 0, 4, 1, 3).reshape(S5_NG, 256, 256)
    prk = pr[S5_BLOCK - 1 - idx][:, :, :, None]
    pik = pi[S5_BLOCK - 1 - idx][:, :, :, None]
    s_in = jnp.concatenate([prk * bbr[None] - pik * bbi[None],
                            prk * bbi[None] + pik * bbr[None]], axis=2)
    s_in = s_in.transpose(1, 0, 3, 2).reshape(S5_NG, 256, 128)
    w1 = jnp.concatenate([toep, s_in], axis=2).astype(BF16)
    s_out = jnp.concatenate([car[1:], -cai[1:]], axis=3)
    cexp = s_out.transpose(1, 3, 0, 2).reshape(S5_NG, 128, 256).astype(BF16)
    p_rot = jnp.concatenate([pr[S5_BLOCK], pr[S5_BLOCK]], axis=-1)
    q_rot = jnp.concatenate([-pi[S5_BLOCK], pi[S5_BLOCK]], axis=-1)
    pq = jnp.stack([p_rot, q_rot, -q_rot])
    return w1, cexp, pq


def kernel(x, w_in, w_out, norm_mix, norm_ffn, conv_w, s5_lam_re, s5_lam_im, s5_b_re, s5_b_im, s5_c_re, s5_c_im, s5_d, s5_log_dt, s5_w_glu, attn_q_norm, attn_k_norm, attn_sinks, hg_lower_bounds, hg_out_norm, group_norm, w_ff1, w_ff2):
    bsz, seq, _ = x.shape
    depth = w_in.shape[0]
    lb_p = jax.nn.softmax(hg_lower_bounds.astype(F32), axis=0)
    lb_table = jnp.cumsum(lb_p, axis=0) - lb_p[0]
    for l in range(depth):
        w1, cexp, pq = _s5_tables(s5_lam_re[l], s5_lam_im[l], s5_b_re[l], s5_b_im[l],
                                  s5_c_re[l], s5_c_im[l], s5_log_dt[l])
        x = _mixer_layer(
            x, attn_sinks[l].astype(F32), norm_mix[l].reshape(1, D_MODEL),
            w_in[l].astype(BF16), w_out[l].astype(BF16), conv_w[l].astype(F32),
            w1, cexp, pq, s5_d[l].reshape(1, GROUP_W), s5_w_glu[l].astype(BF16),
            jnp.tile(attn_q_norm[l], ATT_QH).reshape(1, 256),
            jnp.tile(attn_k_norm[l], ATT_KVH).reshape(1, 128),
            lb_table[l].reshape(1, GROUP_W),
            jnp.tile(hg_out_norm[l], HG_HEADS).reshape(1, GROUP_W),
            group_norm[l].reshape(1, D_MODEL))
        x = _ffn_layer(x.reshape(bsz * seq, D_MODEL), norm_ffn[l].reshape(1, D_MODEL),
                       w_ff1[l].astype(BF16), w_ff2[l].astype(BF16)).reshape(bsz, seq, D_MODEL)
    return x
```

```python
import functools

import numpy as np
import jax
import jax.numpy as jnp
from jax import lax
from jax.experimental import pallas as pl
from jax.experimental.pallas import tpu as pltpu

F32 = jnp.float32
BF16 = jnp.bfloat16

D_MODEL = 1024
GROUP_W = 256
N_GROUPS = 4
CHUNK = 64
CONV_W = 3
S5_GROUP = 16
S5_NG = 16
S5_STATE = 64
S5_BLOCK = 16
ATT_HD = 64
ATT_QH = 4
ATT_KVH = 2
ATT_REP = 2
ATT_SCALE = ATT_HD ** -0.5
WIN_CHUNKS = 2
HG_HEADS = 4
HG_DK = 64
HG_BLOCK = 16
D_FF = 4096
D_IN = 2560
EPS = 1e-6
LANES = 128

J_CVH, J_CVB, J_CVC, J_S5U = 0, 2, 4, 6
J_AQ, J_AK, J_AV = 8, 10, 11
J_HQ, J_HF, J_HI, J_HG = 12, 14, 16, 18
N_CHUNKS = D_IN // LANES

SEQ_TILE = 512
FFN_TILE = 512
FFN_COLS = 1024
VMEM_LIMIT_BYTES = 56 * 1024 * 1024


def _dot(a, b):
    return jnp.dot(a, b, preferred_element_type=F32)


def _dot_nt(a, b):
    return lax.dot_general(a, b, (((1,), (1,)), ((), ())), preferred_element_type=F32)


def _dot_tn(a, b):
    return lax.dot_general(a, b, (((0,), (0,)), ((), ())), preferred_element_type=F32)


def _segment_ones(n, seg):
    r = lax.broadcasted_iota(jnp.int32, (n, n), 0) // seg
    c = lax.broadcasted_iota(jnp.int32, (n, n), 1) // seg
    return (r == c).astype(BF16)


def _chunk_transpose16(arrs):
    rows = arrs[0].shape[0]
    chunk = lax.broadcasted_iota(jnp.int32, (rows, 256), 1) // 16
    for j in range(4):
        s = 1 << j
        low = ((chunk >> j) & 1) == 0
        new = list(arrs)
        for i in range(16):
            if i & s:
                continue
            a, b = arrs[i], arrs[i + s]
            new[i] = jnp.where(low, a, pltpu.roll(b, 16 * s, axis=1))
            new[i + s] = jnp.where(low, pltpu.roll(a, 256 - 16 * s, axis=1), b)
        arrs = new
    return arrs


def _halves(ref, base):
    return jnp.concatenate([ref[base], ref[base + 1]], axis=1)


def _mixer_kernel(sinks_ref, x_ref, nm_ref, win_ref, wout_ref, convw_ref, w1_ref, cexp_ref,
                  pq_ref, dskip_ref, wglu_ref, qg_ref, kg_ref, lb_ref, hgn_ref, gn_ref,
                  o_ref,
                  proj_sc, zp_sc, kext_sc, vext_sc, hgs_sc, hgo_sc, hgx_sc, hgi_sc, hgy_sc,
                  s5s_sc, s5c_sc, s5w_sc, s5i_sc, s5y_sc, s5o_sc, att_sc, ycat_sc,
                  *, tt):
    t = pl.program_id(1)
    nb = tt // S5_BLOCK
    nhb = tt // HG_BLOCK
    ncp = tt // CHUNK
    band = (WIN_CHUNKS + 1) * CHUNK
    hist = WIN_CHUNKS * CHUNK

    @pl.when(t == 0)
    def _reset():
        zp_sc[0:8, :] = jnp.zeros((8, GROUP_W), F32)
        kext_sc[0:hist, :] = jnp.zeros((hist, LANES), BF16)
        vext_sc[0:hist, :] = jnp.zeros((hist, LANES), BF16)
        hgs_sc[...] = jnp.zeros((HG_DK, GROUP_W), F32)
        s5s_sc[...] = jnp.zeros((2, S5_NG, LANES), F32)

    x = x_ref[0]
    ms = jnp.mean(x * x, axis=-1, keepdims=True)
    hn = (x * lax.rsqrt(ms + EPS) * nm_ref[...]).astype(BF16)
    for c0 in range(0, D_IN, 512):
        res = _dot(hn, win_ref[:, c0:c0 + 512])
        for j in range(4):
            proj_sc[c0 // LANES + j] = res[:, j * LANES:(j + 1) * LANES]

    def group_norm(y, g):
        gain = gn_ref[:, g * GROUP_W:(g + 1) * GROUP_W]
        msq = jnp.mean(y * y, axis=-1, keepdims=True)
        return y * lax.rsqrt(msq + EPS) * gain

    ones256 = _segment_ones(256, 64)
    ones128 = _segment_ones(128, 64)

    z = _halves(proj_sc, J_CVC) * _halves(proj_sc, J_CVH)
    zp_sc[8:8 + tt, :] = z
    acc = (convw_ref[0:1, :] * zp_sc[6:6 + tt, :] + convw_ref[1:2, :] * zp_sc[7:7 + tt, :]
           + convw_ref[2:3, :] * z)
    y_a = _halves(proj_sc, J_CVB) * acc
    zp_sc[0:8, :] = zp_sc[tt:tt + 8, :]
    ycat_sc[:, 0:GROUP_W] = group_norm(y_a, 0).astype(BF16)

    folded = [jnp.concatenate([proj_sc[J_S5U + half, pl.ds(k, nb, stride=S5_BLOCK), :]
                               for half in range(2)], axis=1)
              for k in range(S5_BLOCK)]
    per_group = _chunk_transpose16(folded)
    for g in range(S5_NG):
        zg = _dot(per_group[g].astype(BF16), w1_ref[g])
        s5y_sc[g] = zg[:, 0:GROUP_W]
        s5c_sc[pl.ds(g, nb, stride=S5_NG), :] = zg[:, GROUP_W:GROUP_W + LANES]
    s5w_sc[...] = pltpu.roll(s5c_sc[...], 64, axis=1)
    p_rot, q_rot, q_swp = pq_ref[0], pq_ref[1], pq_ref[2]
    st, sw = s5s_sc[0], s5s_sc[1]
    for b in range(nb):
        s5i_sc[b * S5_NG:(b + 1) * S5_NG, :] = st
        c_b = s5c_sc[b * S5_NG:(b + 1) * S5_NG, :]
        w_b = s5w_sc[b * S5_NG:(b + 1) * S5_NG, :]
        st, sw = (p_rot * st + q_rot * sw + c_b, p_rot * sw + q_swp * st + w_b)
    s5s_sc[0] = st
    s5s_sc[1] = sw
    outs = []
    for g in range(S5_NG):
        s_in = s5i_sc[pl.ds(g, nb, stride=S5_NG), :]
        outs.append(s5y_sc[g] + _dot(s_in.astype(BF16), cexp_ref[g]))
    unfolded = _chunk_transpose16(outs)
    for k in range(S5_BLOCK):
        for half in range(2):
            s5o_sc[half, pl.ds(k, nb, stride=S5_BLOCK), :] = (
                unfolded[k][:, LANES * half:LANES * (half + 1)])
    y5 = _halves(s5o_sc, 0) + dskip_ref[...] * _halves(proj_sc, J_S5U)
    y5 = jax.nn.gelu(y5)
    glu = _dot(y5.astype(BF16), wglu_ref[...])
    y_b = glu[:, 0:GROUP_W] * jax.nn.sigmoid(glu[:, GROUP_W:2 * GROUP_W])
    ycat_sc[:, GROUP_W:2 * GROUP_W] = group_norm(y_b, 1).astype(BF16)

    aq = _halves(proj_sc, J_AQ)
    ak = proj_sc[J_AK]
    q_ms = _dot((aq * aq).astype(BF16), ones256) * (1.0 / ATT_HD)
    k_ms = _dot((ak * ak).astype(BF16), ones128) * (1.0 / ATT_HD)
    qn = aq * lax.rsqrt(q_ms + EPS) * qg_ref[...] * ATT_SCALE
    kn = ak * lax.rsqrt(k_ms + EPS) * kg_ref[...]
    kext_sc[hist:hist + tt, :] = kn.astype(BF16)
    vext_sc[hist:hist + tt, :] = proj_sc[J_AV].astype(BF16)
    lane_kv = lax.broadcasted_iota(jnp.int32, (tt, LANES), 1) // ATT_HD
    q_masked = [[jnp.where(lane_kv == g, qn[:, r * LANES:(r + 1) * LANES], 0.0).astype(BF16)
                 for r in range(ATT_REP)] for g in range(ATT_KVH)]
    col = lax.broadcasted_iota(jnp.int32, (ATT_REP * CHUNK, band), 1)
    row = lax.broadcasted_iota(jnp.int32, (ATT_REP * CHUNK, 1), 0)
    first_kv = lax.broadcasted_iota(jnp.int32, (CHUNK, LANES), 1) < ATT_HD
    blocks = [(c, g) for c in range(ncp) for g in range(ATT_KVH)]
    scores = []
    for c, g in blocks:
        q2 = jnp.concatenate([q_masked[g][r][c * CHUNK:(c + 1) * CHUNK, :] for r in range(ATT_REP)],
                             axis=0)
        scores.append(_dot_nt(q2, kext_sc[c * CHUNK:c * CHUNK + band, :]))
    probs, dens = [], []
    for (c, g), s in zip(blocks, scores):
        if c < WIN_CHUNKS:
            first_valid = jnp.maximum(WIN_CHUNKS - (t * ncp + c), 0) * CHUNK
            s = jnp.where(col >= first_valid, s, -jnp.inf)
        sink = jnp.where(row < CHUNK, sinks_ref[ATT_REP * g], sinks_ref[ATT_REP * g + 1])
        m = jnp.maximum(jnp.max(s, axis=-1, keepdims=True), sink)
        p = jnp.exp(s - m)
        dens.append(jnp.sum(p, axis=-1, keepdims=True) + jnp.exp(sink - m))
        probs.append(p.astype(BF16))
    outs_kv = {}
    for (c, g), p, den in zip(blocks, probs, dens):
        outs_kv[c, g] = _dot(p, vext_sc[c * CHUNK:c * CHUNK + band, :]) / den
    for c in range(ncp):
        for r in range(ATT_REP):
            att_sc[c * CHUNK:(c + 1) * CHUNK, r * LANES:(r + 1) * LANES] = jnp.where(
                first_kv, outs_kv[c, 0][r * CHUNK:(r + 1) * CHUNK, :],
                outs_kv[c, 1][r * CHUNK:(r + 1) * CHUNK, :])
    kext_sc[0:hist, :] = kext_sc[tt:tt + hist, :]
    vext_sc[0:hist, :] = vext_sc[tt:tt + hist, :]
    ycat_sc[:, 2 * GROUP_W:3 * GROUP_W] = group_norm(att_sc[...], 2).astype(BF16)

    def pos_major(j):
        return jnp.concatenate(
            [jnp.concatenate([proj_sc[j + half, pl.ds(p, nhb, stride=HG_BLOCK), :]
                              for p in range(HG_BLOCK)], axis=0)
             for half in range(2)], axis=1)

    hq = pos_major(J_HQ)
    vv = pos_major(J_HI)
    gate = pos_major(J_HG)
    lb = lb_ref[...]
    fg = lb + (1.0 - lb) * jax.nn.sigmoid(pos_major(J_HF))
    kk = 1.0 - fg
    lf = jnp.log(fg)
    cum = [lf[0:nhb, :]]
    for p in range(1, HG_BLOCK):
        cum.append(cum[-1] + lf[p * nhb:(p + 1) * nhb, :])
    bc = jnp.concatenate(cum, axis=0)
    b_last = cum[-1]
    gk = jnp.log(kk) - bc
    hgo_sc[...] = _dot((hq * kk).astype(BF16), ones256) * vv
    for d in range(1, HG_BLOCK):
        r0 = d * nhb
        n = tt - r0
        e = hq[r0:, :] * jnp.exp(bc[r0:, :] + gk[0:n, :])
        hgo_sc[r0:tt, :] += _dot(e.astype(BF16), ones256) * vv[0:n, :]
    qd = hq * jnp.exp(bc)
    k_end = kk * jnp.exp(jnp.concatenate([b_last] * HG_BLOCK, axis=0) - bc)
    for half in range(2):
        sl = slice(half * LANES, (half + 1) * LANES)
        hgx_sc[0 + half] = qd[:, sl]
        hgx_sc[2 + half] = k_end[:, sl]
        hgx_sc[4 + half] = vv[:, sl]
    lane_head = lax.broadcasted_iota(jnp.int32, (HG_BLOCK, GROUP_W), 1) // HG_DK
    head_masks = [lane_head == h for h in range(HG_HEADS)]
    dec = jnp.exp(b_last)
    def block_rows(i, n):
        return jnp.concatenate(
            [hgx_sc[2 * i + half, pl.ds(n, HG_BLOCK, stride=nhb), :] for half in range(2)], axis=1)

    def per_head_rows(a):
        return jnp.concatenate([jnp.where(m, a, 0.0) for m in head_masks], axis=0).astype(BF16)

    upds = []
    for n in range(nhb):
        v_n = block_rows(2, n)
        v_stk = jnp.concatenate([v_n[:, h * HG_DK:(h + 1) * HG_DK] for h in range(HG_HEADS)], axis=0)
        upds.append(_dot_tn(v_stk.astype(BF16), per_head_rows(block_rows(1, n))))
    s_val = hgs_sc[...]
    states = []
    for n in range(nhb):
        states.append(s_val.astype(BF16))
        s_val = s_val * dec[n:n + 1, :] + upds[n]
    hgs_sc[...] = s_val
    for n in range(nhb):
        out = _dot_nt(per_head_rows(block_rows(0, n)), states[n])
        o_n = jnp.concatenate([out[h * HG_BLOCK:(h + 1) * HG_BLOCK, :] for h in range(HG_HEADS)],
                              axis=1)
        for half in range(2):
            hgi_sc[half, pl.ds(n, HG_BLOCK, stride=nhb), :] = o_n[:, half * LANES:(half + 1) * LANES]
    o_h = hgo_sc[...] + _halves(hgi_sc, 0)
    o_ms = _dot((o_h * o_h).astype(BF16), ones256) * (1.0 / HG_DK)
    y_d = o_h * lax.rsqrt(o_ms + EPS) * hgn_ref[...] * (gate * jax.nn.sigmoid(gate))
    y_d = group_norm(y_d, 3)
    for p in range(HG_BLOCK):
        for half in range(2):
            hgy_sc[half, pl.ds(p, nhb, stride=HG_BLOCK), :] = (
                y_d[p * nhb:(p + 1) * nhb, half * LANES:(half + 1) * LANES])
    ycat_sc[:, 3 * GROUP_W:4 * GROUP_W] = _halves(hgy_sc, 0).astype(BF16)

    o_ref[0] = x + _dot(ycat_sc[...], wout_ref[...])


def _mixer_layer(x, sinks, nm, w_in, w_out, conv_w, w1, cexp, pq, dskip, wglu, qg, kg, lb, hgn, gn):
    bsz, seq, _ = x.shape
    tt = min(SEQ_TILE, seq)
    assert seq % tt == 0 and tt % CHUNK == 0 and tt >= WIN_CHUNKS * CHUNK
    nb = tt // S5_BLOCK
    assert nb % 8 == 0
    const2 = lambda b, t: (0, 0)
    const3 = lambda b, t: (0, 0, 0)
    in_specs = [
        pl.BlockSpec(memory_space=pltpu.SMEM),
        pl.BlockSpec((1, tt, D_MODEL), lambda b, t: (b, t, 0)),
        pl.BlockSpec((1, D_MODEL), const2),
        pl.BlockSpec((D_MODEL, D_IN), const2),
        pl.BlockSpec((D_MODEL, D_MODEL), const2),
        pl.BlockSpec((CONV_W, GROUP_W), const2),
        pl.BlockSpec((S5_NG, 256, 384), const3),
        pl.BlockSpec((S5_NG, 128, 256), const3),
        pl.BlockSpec((3, S5_NG, 128), const3),
        pl.BlockSpec((1, GROUP_W), const2),
        pl.BlockSpec((GROUP_W, 2 * GROUP_W), const2),
        pl.BlockSpec((1, 256), const2),
        pl.BlockSpec((1, 128), const2),
        pl.BlockSpec((1, GROUP_W), const2),
        pl.BlockSpec((1, GROUP_W), const2),
        pl.BlockSpec((1, D_MODEL), const2),
    ]
    scratch = [
        pltpu.VMEM((N_CHUNKS, tt, LANES), F32),
        pltpu.VMEM((8 + tt, GROUP_W), F32),
        pltpu.VMEM((WIN_CHUNKS * CHUNK + tt, LANES), BF16),
        pltpu.VMEM((WIN_CHUNKS * CHUNK + tt, LANES), BF16),
        pltpu.VMEM((HG_DK, GROUP_W), F32),
        pltpu.VMEM((tt, GROUP_W), F32),
        pltpu.VMEM((6, tt, LANES), F32),
        pltpu.VMEM((2, tt, LANES), F32),
        pltpu.VMEM((2, tt, LANES), F32),
        pltpu.VMEM((2, S5_NG, LANES), F32),
        pltpu.VMEM((nb * S5_NG, LANES), F32),
        pltpu.VMEM((nb * S5_NG, LANES), F32),
        pltpu.VMEM((nb * S5_NG, LANES), F32),
        pltpu.VMEM((S5_NG, nb, GROUP_W), F32),
        pltpu.VMEM((2, tt, LANES), F32),
        pltpu.VMEM((tt, GROUP_W), F32),
        pltpu.VMEM((tt, D_MODEL), BF16),
    ]
    return pl.pallas_call(
        functools.partial(_mixer_kernel, tt=tt),
        grid=(bsz, seq // tt),
        in_specs=in_specs,
        out_specs=pl.BlockSpec((1, tt, D_MODEL), lambda b, t: (b, t, 0)),
        out_shape=jax.ShapeDtypeStruct(x.shape, x.dtype),
        scratch_shapes=scratch,
        compiler_params=pltpu.CompilerParams(
            dimension_semantics=("parallel", "arbitrary"),
            vmem_limit_bytes=VMEM_LIMIT_BYTES),
        name="mixer_layer",
    )(sinks, x, nm, w_in, w_out, conv_w, w1, cexp, pq, dskip, wglu, qg, kg, lb, hgn, gn)


def _ffn_kernel(x_ref, nf_ref, w1_ref, w2_ref, o_ref, acc_sc):
    x = x_ref[...]
    ms = jnp.mean(x * x, axis=-1, keepdims=True)
    hn = (x * lax.rsqrt(ms + EPS) * nf_ref[...]).astype(BF16)
    for j in range(D_FF // FFN_COLS):
        a = jnp.maximum(_dot(hn, w1_ref[:, j * FFN_COLS:(j + 1) * FFN_COLS]), 0.0)
        part = _dot((a * a).astype(BF16), w2_ref[j * FFN_COLS:(j + 1) * FFN_COLS, :])
        if j == 0:
            acc_sc[...] = x + part
        else:
            acc_sc[...] += part
    o_ref[...] = acc_sc[...]


def _ffn_layer(x2, nf, w1, w2):
    rows = x2.shape[0]
    tm = min(FFN_TILE, rows)
    assert rows % tm == 0
    return pl.pallas_call(
        _ffn_kernel,
        grid=(rows // tm,),
        in_specs=[
            pl.BlockSpec((tm, D_MODEL), lambda i: (i, 0)),
            pl.BlockSpec((1, D_MODEL), lambda i: (0, 0)),
            pl.BlockSpec((D_MODEL, D_FF), lambda i: (0, 0)),
            pl.BlockSpec((D_FF, D_MODEL), lambda i: (0, 0)),
        ],
        out_specs=pl.BlockSpec((tm, D_MODEL), lambda i: (i, 0)),
        out_shape=jax.ShapeDtypeStruct(x2.shape, x2.dtype),
        scratch_shapes=[pltpu.VMEM((tm, D_MODEL), F32)],
        compiler_params=pltpu.CompilerParams(
            dimension_semantics=("parallel",),
            vmem_limit_bytes=VMEM_LIMIT_BYTES),
        name="ffn_layer",
    )(x2, nf, w1, w2)


def _s5_tables(lam_re, lam_im, b_re, b_im, c_re, c_im, log_dt):
    hp = lax.Precision.HIGHEST
    lr = jnp.minimum(lam_re.astype(F32), -1e-4)
    li = lam_im.astype(F32)
    dt = jnp.exp(log_dt.astype(F32))[:, None]
    mag = jnp.exp(lr * dt)
    ar = mag * jnp.cos(li * dt)
    ai = mag * jnp.sin(li * dt)
    den = lr * lr + li * li
    zr = ((ar - 1.0) * lr + ai * li) / den
    zi = (ai * lr - (ar - 1.0) * li) / den
    bre = b_re.astype(F32)
    bim = b_im.astype(F32)
    bbr = zr[..., None] * bre - zi[..., None] * bim
    bbi = zr[..., None] * bim + zi[..., None] * bre
    d = jnp.arange(S5_BLOCK + 1, dtype=F32)[:, None, None]
    magd = jnp.exp(d * (lr * dt))
    pr = magd * jnp.cos(d * (li * dt))
    pi = magd * jnp.sin(d * (li * dt))
    cre = c_re.astype(F32)[None]
    cim = c_im.astype(F32)[None]
    car = cre * pr[:, :, None, :] - cim * pi[:, :, None, :]
    cai = cre * pi[:, :, None, :] + cim * pr[:, :, None, :]
    kern = (jnp.einsum('dghp,gpj->dghj', car[:S5_BLOCK], bbr, precision=hp)
            - jnp.einsum('dghp,gpj->dghj', cai[:S5_BLOCK], bbi, precision=hp))
    idx = jnp.arange(S5_BLOCK)
    lag = idx[None, :] - idx[:, None]
    toep = jnp.where((lag >= 0)[:, :, None, None, None], kern[jnp.clip(lag, 0, S5_BLOCK - 1)], 0.0)
    toep = toep.transpose(2, 0, 4, 1, 3).reshape(S5_NG, 256, 256)
    prk = pr[S5_BLOCK - 1 - idx][:, :, :, None]
    pik = pi[S5_BLOCK - 1 - idx][:, :, :, None]
    s_in = jnp.concatenate([prk * bbr[None] - pik * bbi[None],
                            prk * bbi[None] + pik * bbr[None]], axis=2)
    s_in = s_in.transpose(1, 0, 3, 2).reshape(S5_NG, 256, 128)
    w1 = jnp.concatenate([toep, s_in], axis=2).astype(BF16)
    s_out = jnp.concatenate([car[1:], -cai[1:]], axis=3)
    cexp = s_out.transpose(1, 3, 0, 2).reshape(S5_NG, 128, 256).astype(BF16)
    p_rot = jnp.concatenate([pr[S5_BLOCK], pr[S5_BLOCK]], axis=-1)
    q_rot = jnp.concatenate([-pi[S5_BLOCK], pi[S5_BLOCK]], axis=-1)
    pq = jnp.stack([p_rot, q_rot, -q_rot])
    return w1, cexp, pq


def kernel(x, w_in, w_out, norm_mix, norm_ffn, conv_w, s5_lam_re, s5_lam_im, s5_b_re, s5_b_im, s5_c_re, s5_c_im, s5_d, s5_log_dt, s5_w_glu, attn_q_norm, attn_k_norm, attn_sinks, hg_lower_bounds, hg_out_norm, group_norm, w_ff1, w_ff2):
    bsz, seq, _ = x.shape
    depth = w_in.shape[0]
    lb_p = jax.nn.softmax(hg_lower_bounds.astype(F32), axis=0)
    lb_table = jnp.cumsum(lb_p, axis=0) - lb_p[0]
    head_order = np.array([0, 2, 1, 3])
    head_cols = (head_order[:, None] * ATT_HD + np.arange(ATT_HD)[None, :]).reshape(-1)
    in_cols = np.arange(D_IN)
    in_cols[J_AQ * LANES:J_AQ * LANES + GROUP_W] = J_AQ * LANES + head_cols
    mix_cols = np.arange(D_MODEL)
    mix_cols[2 * GROUP_W:3 * GROUP_W] = 2 * GROUP_W + head_cols
    for l in range(depth):
        w1, cexp, pq = _s5_tables(s5_lam_re[l], s5_lam_im[l], s5_b_re[l], s5_b_im[l],
                                  s5_c_re[l], s5_c_im[l], s5_log_dt[l])
        x = _mixer_layer(
            x, attn_sinks[l].astype(F32), norm_mix[l].reshape(1, D_MODEL),
            w_in[l][:, in_cols].astype(BF16), w_out[l][mix_cols, :].astype(BF16),
            conv_w[l].astype(F32),
            w1, cexp, pq, s5_d[l].reshape(1, GROUP_W), s5_w_glu[l].astype(BF16),
            jnp.tile(attn_q_norm[l], ATT_QH).reshape(1, 256),
            jnp.tile(attn_k_norm[l], ATT_KVH).reshape(1, 128),
            lb_table[l].reshape(1, GROUP_W),
            jnp.tile(hg_out_norm[l], HG_HEADS).reshape(1, GROUP_W),
            group_norm[l][mix_cols].reshape(1, D_MODEL))
        x = _ffn_layer(x.reshape(bsz * seq, D_MODEL), norm_ffn[l].reshape(1, D_MODEL),
                       w_ff1[l].astype(BF16), w_ff2[l].astype(BF16)).reshape(bsz, seq, D_MODEL)
    return x
```

```python
import functools

import jax
import jax.numpy as jnp
from jax import lax
from jax.experimental import pallas as pl
from jax.experimental.pallas import tpu as pltpu

F32 = jnp.float32
BF16 = jnp.bfloat16

D_MODEL = 1024
GROUP_W = 256
N_GROUPS = 4
CHUNK = 64
CONV_W = 3
S5_GROUP = 16
S5_NG = 16
S5_STATE = 64
S5_BLOCK = 16
ATT_HD = 64
ATT_QH = 4
ATT_KVH = 2
ATT_REP = 2
ATT_SCALE = ATT_HD ** -0.5
WIN_CHUNKS = 2
HG_HEADS = 4
HG_DK = 64
HG_BLOCK = 16
D_FF = 4096
D_IN = 2560
EPS = 1e-6
LANES = 128

J_CVH, J_CVB, J_CVC, J_S5U = 0, 2, 4, 6
J_AQ, J_AK, J_AV = 8, 10, 11
J_HQ, J_HF, J_HI, J_HG = 12, 14, 16, 18
N_CHUNKS = D_IN // LANES

SEQ_TILE = 512
FFN_TILE = 512
FFN_COLS = 1024
VMEM_LIMIT_BYTES = 56 * 1024 * 1024


def _dot(a, b):
    return jnp.dot(a, b, preferred_element_type=F32)


def _dot_nt(a, b):
    return lax.dot_general(a, b, (((1,), (1,)), ((), ())), preferred_element_type=F32)


def _dot_tn(a, b):
    return lax.dot_general(a, b, (((0,), (0,)), ((), ())), preferred_element_type=F32)


def _segment_ones(n, seg):
    r = lax.broadcasted_iota(jnp.int32, (n, n), 0) // seg
    c = lax.broadcasted_iota(jnp.int32, (n, n), 1) // seg
    return (r == c).astype(BF16)


def _chunk_transpose16(arrs):
    rows = arrs[0].shape[0]
    chunk = lax.broadcasted_iota(jnp.int32, (rows, 256), 1) // 16
    for j in range(4):
        s = 1 << j
        low = ((chunk >> j) & 1) == 0
        new = list(arrs)
        for i in range(16):
            if i & s:
                continue
            a, b = arrs[i], arrs[i + s]
            new[i] = jnp.where(low, a, pltpu.roll(b, 16 * s, axis=1))
            new[i + s] = jnp.where(low, pltpu.roll(a, 256 - 16 * s, axis=1), b)
        arrs = new
    return arrs


def _halves(ref, base):
    return jnp.concatenate([ref[base], ref[base + 1]], axis=1)


def _mixer_kernel(sinks_ref, x_ref, nm_ref, win_ref, wout_ref, convw_ref, w1_ref, cexp_ref,
                  pq_ref, dskip_ref, wglu_ref, qg_ref, kg_ref, lb_ref, hgn_ref, gn_ref,
                  o_ref,
                  proj_sc, zp_sc, kext_sc, vext_sc, hgs_sc, hgo_sc, hgx_sc, hgi_sc, hgy_sc,
                  s5s_sc, s5c_sc, s5w_sc, s5i_sc, s5y_sc, s5o_sc, att_sc, ycat_sc,
                  *, tt):
    t = pl.program_id(1)
    nb = tt // S5_BLOCK
    nhb = tt // HG_BLOCK
    ncp = tt // CHUNK
    band = (WIN_CHUNKS + 1) * CHUNK
    hist = WIN_CHUNKS * CHUNK

    @pl.when(t == 0)
    def _reset():
        zp_sc[0:8, :] = jnp.zeros((8, GROUP_W), F32)
        kext_sc[0:hist, :] = jnp.zeros((hist, LANES), BF16)
        vext_sc[0:hist, :] = jnp.zeros((hist, LANES), BF16)
        hgs_sc[...] = jnp.zeros((HG_DK, GROUP_W), F32)
        s5s_sc[...] = jnp.zeros((2, S5_NG, LANES), F32)

    x = x_ref[0]
    ms = jnp.mean(x * x, axis=-1, keepdims=True)
    hn = (x * lax.rsqrt(ms + EPS) * nm_ref[...]).astype(BF16)
    for c0 in range(0, D_IN, 512):
        res = _dot(hn, win_ref[:, c0:c0 + 512])
        for j in range(4):
            proj_sc[c0 // LANES + j] = res[:, j * LANES:(j + 1) * LANES]

    def group_norm(y, g):
        gain = gn_ref[:, g * GROUP_W:(g + 1) * GROUP_W]
        msq = jnp.mean(y * y, axis=-1, keepdims=True)
        return y * lax.rsqrt(msq + EPS) * gain

    ones256 = _segment_ones(256, 64)
    ones128 = _segment_ones(128, 64)

    z = _halves(proj_sc, J_CVC) * _halves(proj_sc, J_CVH)
    zp_sc[8:8 + tt, :] = z
    acc = (convw_ref[0:1, :] * zp_sc[6:6 + tt, :] + convw_ref[1:2, :] * zp_sc[7:7 + tt, :]
           + convw_ref[2:3, :] * z)
    y_a = _halves(proj_sc, J_CVB) * acc
    zp_sc[0:8, :] = zp_sc[tt:tt + 8, :]
    ycat_sc[:, 0:GROUP_W] = group_norm(y_a, 0).astype(BF16)

    folded = [jnp.concatenate([proj_sc[J_S5U + half, pl.ds(k, nb, stride=S5_BLOCK), :]
                               for half in range(2)], axis=1)
              for k in range(S5_BLOCK)]
    per_group = _chunk_transpose16(folded)
    for g in range(S5_NG):
        zg = _dot(per_group[g].astype(BF16), w1_ref[g])
        s5y_sc[g] = zg[:, 0:GROUP_W]
        s5c_sc[pl.ds(g, nb, stride=S5_NG), :] = zg[:, GROUP_W:GROUP_W + LANES]
    s5w_sc[...] = pltpu.roll(s5c_sc[...], 64, axis=1)
    p_rot, q_rot, q_swp = pq_ref[0], pq_ref[1], pq_ref[2]
    st, sw = s5s_sc[0], s5s_sc[1]
    for b in range(nb):
        s5i_sc[b * S5_NG:(b + 1) * S5_NG, :] = st
        c_b = s5c_sc[b * S5_NG:(b + 1) * S5_NG, :]
        w_b = s5w_sc[b * S5_NG:(b + 1) * S5_NG, :]
        st, sw = (p_rot * st + q_rot * sw + c_b, p_rot * sw + q_swp * st + w_b)
    s5s_sc[0] = st
    s5s_sc[1] = sw
    outs = []
    for g in range(S5_NG):
        s_in = s5i_sc[pl.ds(g, nb, stride=S5_NG), :]
        outs.append(s5y_sc[g] + _dot(s_in.astype(BF16), cexp_ref[g]))
    unfolded = _chunk_transpose16(outs)
    for k in range(S5_BLOCK):
        for half in range(2):
            s5o_sc[half, pl.ds(k, nb, stride=S5_BLOCK), :] = (
                unfolded[k][:, LANES * half:LANES * (half + 1)])
    y5 = _halves(s5o_sc, 0) + dskip_ref[...] * _halves(proj_sc, J_S5U)
    y5 = jax.nn.gelu(y5)
    glu = _dot(y5.astype(BF16), wglu_ref[...])
    y_b = glu[:, 0:GROUP_W] * jax.nn.sigmoid(glu[:, GROUP_W:2 * GROUP_W])
    ycat_sc[:, GROUP_W:2 * GROUP_W] = group_norm(y_b, 1).astype(BF16)

    aq = _halves(proj_sc, J_AQ)
    ak = proj_sc[J_AK]
    q_ms = _dot((aq * aq).astype(BF16), ones256) * (1.0 / ATT_HD)
    k_ms = _dot((ak * ak).astype(BF16), ones128) * (1.0 / ATT_HD)
    qn = aq * lax.rsqrt(q_ms + EPS) * qg_ref[...] * ATT_SCALE
    kn = ak * lax.rsqrt(k_ms + EPS) * kg_ref[...]
    kext_sc[hist:hist + tt, :] = kn.astype(BF16)
    vext_sc[hist:hist + tt, :] = proj_sc[J_AV].astype(BF16)
    lane_kv = lax.broadcasted_iota(jnp.int32, (tt, LANES), 1) // ATT_HD
    q_masked = [[jnp.where(lane_kv == g, qn[:, r * LANES:(r + 1) * LANES], 0.0).astype(BF16)
                 for r in range(ATT_REP)] for g in range(ATT_KVH)]
    col = lax.broadcasted_iota(jnp.int32, (ATT_REP * CHUNK, band), 1)
    row = lax.broadcasted_iota(jnp.int32, (ATT_REP * CHUNK, 1), 0)
    first_kv = lax.broadcasted_iota(jnp.int32, (CHUNK, LANES), 1) < ATT_HD
    blocks = [(c, g) for c in range(ncp) for g in range(ATT_KVH)]
    scores = []
    for c, g in blocks:
        q2 = jnp.concatenate([q_masked[g][r][c * CHUNK:(c + 1) * CHUNK, :] for r in range(ATT_REP)],
                             axis=0)
        scores.append(_dot_nt(q2, kext_sc[c * CHUNK:c * CHUNK + band, :]))
    probs, dens = [], []
    for (c, g), s in zip(blocks, scores):
        if c < WIN_CHUNKS:
            first_valid = jnp.maximum(WIN_CHUNKS - (t * ncp + c), 0) * CHUNK
            s = jnp.where(col >= first_valid, s, -jnp.inf)
        sink = jnp.where(row < CHUNK, sinks_ref[ATT_REP * g], sinks_ref[ATT_REP * g + 1])
        m = jnp.maximum(jnp.max(s, axis=-1, keepdims=True), sink)
        p = jnp.exp(s - m)
        dens.append(jnp.sum(p, axis=-1, keepdims=True) + jnp.exp(sink - m))
        probs.append(p.astype(BF16))
    outs_kv = {}
    for (c, g), p, den in zip(blocks, probs, dens):
        outs_kv[c, g] = _dot(p, vext_sc[c * CHUNK:c * CHUNK + band, :]) / den
    for c in range(ncp):
        for r in range(ATT_REP):
            att_sc[c * CHUNK:(c + 1) * CHUNK, r * LANES:(r + 1) * LANES] = jnp.where(
                first_kv, outs_kv[c, 0][r * CHUNK:(r + 1) * CHUNK, :],
                outs_kv[c, 1][r * CHUNK:(r + 1) * CHUNK, :])
    kext_sc[0:hist, :] = kext_sc[tt:tt + hist, :]
    vext_sc[0:hist, :] = vext_sc[tt:tt + hist, :]
    ycat_sc[:, 2 * GROUP_W:3 * GROUP_W] = group_norm(att_sc[...], 2).astype(BF16)

    def pos_major(j):
        return jnp.concatenate(
            [jnp.concatenate([proj_sc[j + half, pl.ds(p, nhb, stride=HG_BLOCK), :]
                              for p in range(HG_BLOCK)], axis=0)
             for half in range(2)], axis=1)

    hq = pos_major(J_HQ)
    vv = pos_major(J_HI)
    gate = pos_major(J_HG)
    lb = lb_ref[...]
    fg = lb + (1.0 - lb) * jax.nn.sigmoid(pos_major(J_HF))
    kk = 1.0 - fg
    lf = jnp.log(fg)
    cum = [lf[0:nhb, :]]
    for p in range(1, HG_BLOCK):
        cum.append(cum[-1] + lf[p * nhb:(p + 1) * nhb, :])
    bc = jnp.concatenate(cum, axis=0)
    b_last = cum[-1]
    gk = jnp.log(kk) - bc
    for p in range(HG_BLOCK):
        m = (p + 1) * nhb
        q_p = hq[p * nhb:m, :][None]
        b_p = bc[p * nhb:m, :][None]
        e = q_p * jnp.exp(b_p + gk[0:m, :].reshape(p + 1, nhb, GROUP_W))
        sc = _dot(e.reshape(m, GROUP_W).astype(BF16), ones256) * vv[0:m, :]
        hgo_sc[p * nhb:m, :] = jnp.sum(sc.reshape(p + 1, nhb, GROUP_W), axis=0)
    qd = hq * jnp.exp(bc)
    k_end = kk * jnp.exp(jnp.concatenate([b_last] * HG_BLOCK, axis=0) - bc)
    for half in range(2):
        sl = slice(half * LANES, (half + 1) * LANES)
        hgx_sc[0 + half] = qd[:, sl]
        hgx_sc[2 + half] = k_end[:, sl]
        hgx_sc[4 + half] = vv[:, sl]
    lane_head = lax.broadcasted_iota(jnp.int32, (HG_BLOCK, GROUP_W), 1) // HG_DK
    head_masks = [lane_head == h for h in range(HG_HEADS)]
    dec = jnp.exp(b_last)
    def block_rows(i, n):
        return jnp.concatenate(
            [hgx_sc[2 * i + half, pl.ds(n, HG_BLOCK, stride=nhb), :] for half in range(2)], axis=1)

    def per_head_rows(a):
        return jnp.concatenate([jnp.where(m, a, 0.0) for m in head_masks], axis=0).astype(BF16)

    upds = []
    for n in range(nhb):
        v_n = block_rows(2, n)
        v_stk = jnp.concatenate([v_n[:, h * HG_DK:(h + 1) * HG_DK] for h in range(HG_HEADS)], axis=0)
        upds.append(_dot_tn(v_stk.astype(BF16), per_head_rows(block_rows(1, n))))
    s_val = hgs_sc[...]
    states = []
    for n in range(nhb):
        states.append(s_val.astype(BF16))
        s_val = s_val * dec[n:n + 1, :] + upds[n]
    hgs_sc[...] = s_val
    for n in range(nhb):
        out = _dot_nt(per_head_rows(block_rows(0, n)), states[n])
        o_n = jnp.concatenate([out[h * HG_BLOCK:(h + 1) * HG_BLOCK, :] for h in range(HG_HEADS)],
                              axis=1)
        for half in range(2):
            hgi_sc[half, pl.ds(n, HG_BLOCK, stride=nhb), :] = o_n[:, half * LANES:(half + 1) * LANES]
    o_h = hgo_sc[...] + _halves(hgi_sc, 0)
    o_ms = _dot((o_h * o_h).astype(BF16), ones256) * (1.0 / HG_DK)
    y_d = o_h * lax.rsqrt(o_ms + EPS) * hgn_ref[...] * (gate * jax.nn.sigmoid(gate))
    y_d = group_norm(y_d, 3)
    for p in range(HG_BLOCK):
        for half in range(2):
            hgy_sc[half, pl.ds(p, nhb, stride=HG_BLOCK), :] = (
                y_d[p * nhb:(p + 1) * nhb, half * LANES:(half + 1) * LANES])
    ycat_sc[:, 3 * GROUP_W:4 * GROUP_W] = _halves(hgy_sc, 0).astype(BF16)

    o_ref[0] = x + _dot(ycat_sc[...], wout_ref[...])


def _mixer_layer(x, sinks, nm, w_in, w_out, conv_w, w1, cexp, pq, dskip, wglu, qg, kg, lb, hgn, gn):
    bsz, seq, _ = x.shape
    tt = min(SEQ_TILE, seq)
    assert seq % tt == 0 and tt % CHUNK == 0 and tt >= WIN_CHUNKS * CHUNK
    nb = tt // S5_BLOCK
    assert nb % 8 == 0
    const2 = lambda b, t: (0, 0)
    const3 = lambda b, t: (0, 0, 0)
    in_specs = [
        pl.BlockSpec(memory_space=pltpu.SMEM),
        pl.BlockSpec((1, tt, D_MODEL), lambda b, t: (b, t, 0)),
        pl.BlockSpec((1, D_MODEL), const2),
        pl.BlockSpec((D_MODEL, D_IN), const2),
        pl.BlockSpec((D_MODEL, D_MODEL), const2),
        pl.BlockSpec((CONV_W, GROUP_W), const2),
        pl.BlockSpec((S5_NG, 256, 384), const3),
        pl.BlockSpec((S5_NG, 128, 256), const3),
        pl.BlockSpec((3, S5_NG, 128), const3),
        pl.BlockSpec((1, GROUP_W), const2),
        pl.BlockSpec((GROUP_W, 2 * GROUP_W), const2),
        pl.BlockSpec((1, 256), const2),
        pl.BlockSpec((1, 128), const2),
        pl.BlockSpec((1, GROUP_W), const2),
        pl.BlockSpec((1, GROUP_W), const2),
        pl.BlockSpec((1, D_MODEL), const2),
    ]
    scratch = [
        pltpu.VMEM((N_CHUNKS, tt, LANES), F32),
        pltpu.VMEM((8 + tt, GROUP_W), F32),
        pltpu.VMEM((WIN_CHUNKS * CHUNK + tt, LANES), BF16),
        pltpu.VMEM((WIN_CHUNKS * CHUNK + tt, LANES), BF16),
        pltpu.VMEM((HG_DK, GROUP_W), F32),
        pltpu.VMEM((tt, GROUP_W), F32),
        pltpu.VMEM((6, tt, LANES), F32),
        pltpu.VMEM((2, tt, LANES), F32),
        pltpu.VMEM((2, tt, LANES), F32),
        pltpu.VMEM((2, S5_NG, LANES), F32),
        pltpu.VMEM((nb * S5_NG, LANES), F32),
        pltpu.VMEM((nb * S5_NG, LANES), F32),
        pltpu.VMEM((nb * S5_NG, LANES), F32),
        pltpu.VMEM((S5_NG, nb, GROUP_W), F32),
        pltpu.VMEM((2, tt, LANES), F32),
        pltpu.VMEM((tt, GROUP_W), F32),
        pltpu.VMEM((tt, D_MODEL), BF16),
    ]
    return pl.pallas_call(
        functools.partial(_mixer_kernel, tt=tt),
        grid=(bsz, seq // tt),
        in_specs=in_specs,
        out_specs=pl.BlockSpec((1, tt, D_MODEL), lambda b, t: (b, t, 0)),
        out_shape=jax.ShapeDtypeStruct(x.shape, x.dtype),
        scratch_shapes=scratch,
        compiler_params=pltpu.CompilerParams(
            dimension_semantics=("parallel", "arbitrary"),
            vmem_limit_bytes=VMEM_LIMIT_BYTES),
        name="mixer_layer",
    )(sinks, x, nm, w_in, w_out, conv_w, w1, cexp, pq, dskip, wglu, qg, kg, lb, hgn, gn)


def _ffn_kernel(x_ref, nf_ref, w1_ref, w2_ref, o_ref, acc_sc):
    x = x_ref[...]
    ms = jnp.mean(x * x, axis=-1, keepdims=True)
    hn = (x * lax.rsqrt(ms + EPS) * nf_ref[...]).astype(BF16)
    for j in range(D_FF // FFN_COLS):
        a = jnp.maximum(_dot(hn, w1_ref[:, j * FFN_COLS:(j + 1) * FFN_COLS]), 0.0)
        part = _dot((a * a).astype(BF16), w2_ref[j * FFN_COLS:(j + 1) * FFN_COLS, :])
        if j == 0:
            acc_sc[...] = x + part
        else:
            acc_sc[...] += part
    o_ref[...] = acc_sc[...]


def _ffn_layer(x2, nf, w1, w2):
    rows = x2.shape[0]
    tm = min(FFN_TILE, rows)
    assert rows % tm == 0
    return pl.pallas_call(
        _ffn_kernel,
        grid=(rows // tm,),
        in_specs=[
            pl.BlockSpec((tm, D_MODEL), lambda i: (i, 0)),
            pl.BlockSpec((1, D_MODEL), lambda i: (0, 0)),
            pl.BlockSpec((D_MODEL, D_FF), lambda i: (0, 0)),
            pl.BlockSpec((D_FF, D_MODEL), lambda i: (0, 0)),
        ],
        out_specs=pl.BlockSpec((tm, D_MODEL), lambda i: (i, 0)),
        out_shape=jax.ShapeDtypeStruct(x2.shape, x2.dtype),
        scratch_shapes=[pltpu.VMEM((tm, D_MODEL), F32)],
        compiler_params=pltpu.CompilerParams(
            dimension_semantics=("parallel",),
            vmem_limit_bytes=VMEM_LIMIT_BYTES),
        name="ffn_layer",
    )(x2, nf, w1, w2)


def _s5_tables(lam_re, lam_im, b_re, b_im, c_re, c_im, log_dt):
    hp = lax.Precision.HIGHEST
    lr = jnp.minimum(lam_re.astype(F32), -1e-4)
    li = lam_im.astype(F32)
    dt = jnp.exp(log_dt.astype(F32))[:, None]
    mag = jnp.exp(lr * dt)
    ar = mag * jnp.cos(li * dt)
    ai = mag * jnp.sin(li * dt)
    den = lr * lr + li * li
    zr = ((ar - 1.0) * lr + ai * li) / den
    zi = (ai * lr - (ar - 1.0) * li) / den
    bre = b_re.astype(F32)
    bim = b_im.astype(F32)
    bbr = zr[..., None] * bre - zi[..., None] * bim
    bbi = zr[..., None] * bim + zi[..., None] * bre
    d = jnp.arange(S5_BLOCK + 1, dtype=F32)[:, None, None]
    magd = jnp.exp(d * (lr * dt))
    pr = magd * jnp.cos(d * (li * dt))
    pi = magd * jnp.sin(d * (li * dt))
    cre = c_re.astype(F32)[None]
    cim = c_im.astype(F32)[None]
    car = cre * pr[:, :, None, :] - cim * pi[:, :, None, :]
    cai = cre * pi[:, :, None, :] + cim * pr[:, :, None, :]
    kern = (jnp.einsum('dghp,gpj->dghj', car[:S5_BLOCK], bbr, precision=hp)
            - jnp.einsum('dghp,gpj->dghj', cai[:S5_BLOCK], bbi, precision=hp))
    kcat = kern.transpose(1, 3, 0, 2).reshape(S5_NG, S5_GROUP, 256)
    lane = jnp.arange(256)
    toep = jnp.stack([jnp.where(lane >= S5_GROUP * k, jnp.roll(kcat, S5_GROUP * k, axis=2), 0.0)
                      for k in range(S5_BLOCK)], axis=1)
    toep = toep.reshape(S5_NG, 256, 256)
    prk = jnp.flip(pr[:S5_BLOCK], axis=0).transpose(1, 0, 2)[:, :, None, :]
    pik = jnp.flip(pi[:S5_BLOCK], axis=0).transpose(1, 0, 2)[:, :, None, :]
    bbr_t = bbr.transpose(0, 2, 1)[:, None]
    bbi_t = bbi.transpose(0, 2, 1)[:, None]
    s_in = jnp.concatenate([prk * bbr_t - pik * bbi_t, prk * bbi_t + pik * bbr_t], axis=3)
    s_in = s_in.reshape(S5_NG, 256, 128)
    w1 = jnp.concatenate([toep, s_in], axis=2).astype(BF16)
    s_out = jnp.concatenate([car[1:], -cai[1:]], axis=3).transpose(1, 0, 2, 3)
    cexp = jnp.swapaxes(s_out.reshape(S5_NG, 256, 128), 1, 2).astype(BF16)
    p_rot = jnp.concatenate([pr[S5_BLOCK], pr[S5_BLOCK]], axis=-1)
    q_rot = jnp.concatenate([-pi[S5_BLOCK], pi[S5_BLOCK]], axis=-1)
    pq = jnp.stack([p_rot, q_rot, -q_rot])
    return w1, cexp, pq


def kernel(x, w_in, w_out, norm_mix, norm_ffn, conv_w, s5_lam_re, s5_lam_im, s5_b_re, s5_b_im, s5_c_re, s5_c_im, s5_d, s5_log_dt, s5_w_glu, attn_q_norm, attn_k_norm, attn_sinks, hg_lower_bounds, hg_out_norm, group_norm, w_ff1, w_ff2):
    bsz, seq, _ = x.shape
    depth = w_in.shape[0]
    lb_p = jax.nn.softmax(hg_lower_bounds.astype(F32), axis=0)
    lb_table = jnp.cumsum(lb_p, axis=0) - lb_p[0]
    def swap_mid_heads(a, base, axis):
        cuts = [0, base + ATT_HD, base + 2 * ATT_HD, base + 3 * ATT_HD, a.shape[axis]]
        parts = [lax.slice_in_dim(a, cuts[i], cuts[i + 1], axis=axis) for i in range(4)]
        return jnp.concatenate([parts[0], parts[2], parts[1], parts[3]], axis=axis)

    for l in range(depth):
        w1, cexp, pq = _s5_tables(s5_lam_re[l], s5_lam_im[l], s5_b_re[l], s5_b_im[l],
                                  s5_c_re[l], s5_c_im[l], s5_log_dt[l])
        x = _mixer_layer(
            x, attn_sinks[l].astype(F32), norm_mix[l].reshape(1, D_MODEL),
            swap_mid_heads(w_in[l], J_AQ * LANES, 1).astype(BF16),
            swap_mid_heads(w_out[l], 2 * GROUP_W, 0).astype(BF16),
            conv_w[l].astype(F32),
            w1, cexp, pq, s5_d[l].reshape(1, GROUP_W), s5_w_glu[l].astype(BF16),
            jnp.tile(attn_q_norm[l], ATT_QH).reshape(1, 256),
            jnp.tile(attn_k_norm[l], ATT_KVH).reshape(1, 128),
            lb_table[l].reshape(1, GROUP_W),
            jnp.tile(hg_out_norm[l], HG_HEADS).reshape(1, GROUP_W),
            swap_mid_heads(group_norm[l], 2 * GROUP_W, 0).reshape(1, D_MODEL))
        x = _ffn_layer(x.reshape(bsz * seq, D_MODEL), norm_ffn[l].reshape(1, D_MODEL),
                       w_ff1[l].astype(BF16), w_ff2[l].astype(BF16)).reshape(bsz, seq, D_MODEL)
    return x
```

```python
import functools

import jax
import jax.numpy as jnp
from jax import lax
from jax.experimental import pallas as pl
from jax.experimental.pallas import tpu as pltpu

F32 = jnp.float32
BF16 = jnp.bfloat16

D_MODEL = 1024
GROUP_W = 256
N_GROUPS = 4
CHUNK = 64
CONV_W = 3
S5_GROUP = 16
S5_NG = 16
S5_STATE = 64
S5_BLOCK = 16
ATT_HD = 64
ATT_QH = 4
ATT_KVH = 2
ATT_REP = 2
ATT_SCALE = ATT_HD ** -0.5
WIN_CHUNKS = 2
HG_HEADS = 4
HG_DK = 64
HG_BLOCK = 16
D_FF = 4096
D_IN = 2560
EPS = 1e-6
LANES = 128

J_CVH, J_CVB, J_CVC, J_S5U = 0, 2, 4, 6
J_AQ, J_AK, J_AV = 8, 10, 11
J_HQ, J_HF, J_HI, J_HG = 12, 14, 16, 18
N_CHUNKS = D_IN // LANES

SEQ_TILE = 512
FFN_TILE = 512
FFN_COLS = 1024
VMEM_LIMIT_BYTES = 56 * 1024 * 1024


def _dot(a, b):
    return jnp.dot(a, b, preferred_element_type=F32)


def _dot_nt(a, b):
    return lax.dot_general(a, b, (((1,), (1,)), ((), ())), preferred_element_type=F32)


def _dot_tn(a, b):
    return lax.dot_general(a, b, (((0,), (0,)), ((), ())), preferred_element_type=F32)


def _segment_ones(n, seg):
    r = lax.broadcasted_iota(jnp.int32, (n, n), 0) // seg
    c = lax.broadcasted_iota(jnp.int32, (n, n), 1) // seg
    return (r == c).astype(BF16)


def _chunk_transpose16(arrs):
    rows = arrs[0].shape[0]
    chunk = lax.broadcasted_iota(jnp.int32, (rows, 256), 1) // 16
    for j in range(4):
        s = 1 << j
        low = ((chunk >> j) & 1) == 0
        new = list(arrs)
        for i in range(16):
            if i & s:
                continue
            a, b = arrs[i], arrs[i + s]
            new[i] = jnp.where(low, a, pltpu.roll(b, 16 * s, axis=1))
            new[i + s] = jnp.where(low, pltpu.roll(a, 256 - 16 * s, axis=1), b)
        arrs = new
    return arrs


def _halves(ref, base):
    return jnp.concatenate([ref[base], ref[base + 1]], axis=1)


def _mixer_kernel(sinks_ref, x_ref, nm_ref, win_ref, wout_ref, convw_ref, w1_ref, cexp_ref,
                  pq_ref, dskip_ref, wglu_ref, qg_ref, kg_ref, lb_ref, hgn_ref, gn_ref, perm_ref,
                  o_ref,
                  proj_sc, zp_sc, kext_sc, vext_sc, hgs_sc, hgi_sc,
                  s5s_sc, s5c_sc, s5w_sc, s5i_sc, s5y_sc, s5o_sc, att_sc, ycat_sc,
                  *, tt):
    t = pl.program_id(1)
    nb = tt // S5_BLOCK
    nhb = tt // HG_BLOCK
    ncp = tt // CHUNK
    band = (WIN_CHUNKS + 1) * CHUNK
    hist = WIN_CHUNKS * CHUNK

    @pl.when(t == 0)
    def _reset():
        zp_sc[0:8, :] = jnp.zeros((8, GROUP_W), F32)
        kext_sc[0:hist, :] = jnp.zeros((hist, LANES), BF16)
        vext_sc[0:hist, :] = jnp.zeros((hist, LANES), BF16)
        hgs_sc[...] = jnp.zeros((HG_DK, GROUP_W), F32)
        s5s_sc[...] = jnp.zeros((2, S5_NG, LANES), F32)

    x = x_ref[0]
    ms = jnp.mean(x * x, axis=-1, keepdims=True)
    hn = (x * lax.rsqrt(ms + EPS) * nm_ref[...]).astype(BF16)
    for c0 in range(0, D_IN, 512):
        res = _dot(hn, win_ref[:, c0:c0 + 512])
        for j in range(4):
            proj_sc[c0 // LANES + j] = res[:, j * LANES:(j + 1) * LANES]

    def group_norm(y, g):
        gain = gn_ref[:, g * GROUP_W:(g + 1) * GROUP_W]
        msq = jnp.mean(y * y, axis=-1, keepdims=True)
        return y * lax.rsqrt(msq + EPS) * gain

    ones256 = _segment_ones(256, 64)
    ones128 = _segment_ones(128, 64)

    z = _halves(proj_sc, J_CVC) * _halves(proj_sc, J_CVH)
    zp_sc[8:8 + tt, :] = z
    acc = (convw_ref[0:1, :] * zp_sc[6:6 + tt, :] + convw_ref[1:2, :] * zp_sc[7:7 + tt, :]
           + convw_ref[2:3, :] * z)
    y_a = _halves(proj_sc, J_CVB) * acc
    zp_sc[0:8, :] = zp_sc[tt:tt + 8, :]
    ycat_sc[:, 0:GROUP_W] = group_norm(y_a, 0).astype(BF16)

    folded = [jnp.concatenate([proj_sc[J_S5U + half, pl.ds(k, nb, stride=S5_BLOCK), :]
                               for half in range(2)], axis=1)
              for k in range(S5_BLOCK)]
    per_group = _chunk_transpose16(folded)
    for g in range(S5_NG):
        zg = _dot(per_group[g].astype(BF16), w1_ref[g])
        s5y_sc[g] = zg[:, 0:GROUP_W]
        s5c_sc[pl.ds(g, nb, stride=S5_NG), :] = zg[:, GROUP_W:GROUP_W + LANES]
    s5w_sc[...] = pltpu.roll(s5c_sc[...], 64, axis=1)
    p_rot, q_rot, q_swp = pq_ref[0], pq_ref[1], pq_ref[2]
    st, sw = s5s_sc[0], s5s_sc[1]
    for b in range(nb):
        s5i_sc[b * S5_NG:(b + 1) * S5_NG, :] = st
        c_b = s5c_sc[b * S5_NG:(b + 1) * S5_NG, :]
        w_b = s5w_sc[b * S5_NG:(b + 1) * S5_NG, :]
        st, sw = (p_rot * st + q_rot * sw + c_b, p_rot * sw + q_swp * st + w_b)
    s5s_sc[0] = st
    s5s_sc[1] = sw
    outs = []
    for g in range(S5_NG):
        s_in = s5i_sc[pl.ds(g, nb, stride=S5_NG), :]
        outs.append(s5y_sc[g] + _dot(s_in.astype(BF16), cexp_ref[g]))
    unfolded = _chunk_transpose16(outs)
    for k in range(S5_BLOCK):
        for half in range(2):
            s5o_sc[half, pl.ds(k, nb, stride=S5_BLOCK), :] = (
                unfolded[k][:, LANES * half:LANES * (half + 1)])
    y5 = _halves(s5o_sc, 0) + dskip_ref[...] * _halves(proj_sc, J_S5U)
    y5 = jax.nn.gelu(y5)
    glu = _dot(y5.astype(BF16), wglu_ref[...])
    y_b = glu[:, 0:GROUP_W] * jax.nn.sigmoid(glu[:, GROUP_W:2 * GROUP_W])
    ycat_sc[:, GROUP_W:2 * GROUP_W] = group_norm(y_b, 1).astype(BF16)

    aq = _halves(proj_sc, J_AQ)
    ak = proj_sc[J_AK]
    q_ms = _dot((aq * aq).astype(BF16), ones256) * (1.0 / ATT_HD)
    k_ms = _dot((ak * ak).astype(BF16), ones128) * (1.0 / ATT_HD)
    qn = aq * lax.rsqrt(q_ms + EPS) * qg_ref[...] * ATT_SCALE
    kn = ak * lax.rsqrt(k_ms + EPS) * kg_ref[...]
    kext_sc[hist:hist + tt, :] = kn.astype(BF16)
    vext_sc[hist:hist + tt, :] = proj_sc[J_AV].astype(BF16)
    lane_kv = lax.broadcasted_iota(jnp.int32, (tt, LANES), 1) // ATT_HD
    q_masked = [[jnp.where(lane_kv == g, qn[:, r * LANES:(r + 1) * LANES], 0.0).astype(BF16)
                 for r in range(ATT_REP)] for g in range(ATT_KVH)]
    col = lax.broadcasted_iota(jnp.int32, (ATT_REP * CHUNK, band), 1)
    row = lax.broadcasted_iota(jnp.int32, (ATT_REP * CHUNK, 1), 0)
    first_kv = lax.broadcasted_iota(jnp.int32, (CHUNK, LANES), 1) < ATT_HD
    blocks = [(c, g) for c in range(ncp) for g in range(ATT_KVH)]
    scores = []
    for c, g in blocks:
        q2 = jnp.concatenate([q_masked[g][r][c * CHUNK:(c + 1) * CHUNK, :] for r in range(ATT_REP)],
                             axis=0)
        scores.append(_dot_nt(q2, kext_sc[c * CHUNK:c * CHUNK + band, :]))
    probs, dens = [], []
    for (c, g), s in zip(blocks, scores):
        if c < WIN_CHUNKS:
            first_valid = jnp.maximum(WIN_CHUNKS - (t * ncp + c), 0) * CHUNK
            s = jnp.where(col >= first_valid, s, -jnp.inf)
        sink = jnp.where(row < CHUNK, sinks_ref[ATT_REP * g], sinks_ref[ATT_REP * g + 1])
        m = jnp.maximum(jnp.max(s, axis=-1, keepdims=True), sink)
        p = jnp.exp(s - m)
        dens.append(jnp.sum(p, axis=-1, keepdims=True) + jnp.exp(sink - m))
        probs.append(p.astype(BF16))
    outs_kv = {}
    for (c, g), p, den in zip(blocks, probs, dens):
        outs_kv[c, g] = _dot(p, vext_sc[c * CHUNK:c * CHUNK + band, :]) / den
    for c in range(ncp):
        for r in range(ATT_REP):
            att_sc[c * CHUNK:(c + 1) * CHUNK, r * LANES:(r + 1) * LANES] = jnp.where(
                first_kv, outs_kv[c, 0][r * CHUNK:(r + 1) * CHUNK, :],
                outs_kv[c, 1][r * CHUNK:(r + 1) * CHUNK, :])
    kext_sc[0:hist, :] = kext_sc[tt:tt + hist, :]
    vext_sc[0:hist, :] = vext_sc[tt:tt + hist, :]
    ycat_sc[:, 2 * GROUP_W:3 * GROUP_W] = group_norm(att_sc[...], 2).astype(BF16)

    def pos_major(j):
        return jnp.concatenate(
            [jnp.concatenate([proj_sc[j + half, pl.ds(p, nhb, stride=HG_BLOCK), :]
                              for p in range(HG_BLOCK)], axis=0)
             for half in range(2)], axis=1)

    hq = pos_major(J_HQ)
    vv = pos_major(J_HI)
    lb = lb_ref[...]
    fg = lb + (1.0 - lb) * jax.nn.sigmoid(pos_major(J_HF))
    kk = 1.0 - fg
    lf = jnp.log(fg)
    cum = [lf[0:nhb, :]]
    for p in range(1, HG_BLOCK):
        cum.append(cum[-1] + lf[p * nhb:(p + 1) * nhb, :])
    bc = jnp.concatenate(cum, axis=0)
    b_last = cum[-1]
    gk = jnp.log(kk) - bc
    o_groups = []
    for p in range(HG_BLOCK):
        m = (p + 1) * nhb
        q_p = hq[p * nhb:m, :][None]
        b_p = bc[p * nhb:m, :][None]
        e = q_p * jnp.exp(b_p + gk[0:m, :].reshape(p + 1, nhb, GROUP_W))
        sc = _dot(e.reshape(m, GROUP_W).astype(BF16), ones256) * vv[0:m, :]
        o_groups.append(jnp.sum(sc.reshape(p + 1, nhb, GROUP_W), axis=0))
    o_pm = jnp.concatenate(o_groups, axis=0)
    perm = perm_ref[...]
    qd = _dot(perm, (hq * jnp.exp(bc)).astype(BF16)).astype(BF16)
    k_end = kk * jnp.exp(jnp.concatenate([b_last] * HG_BLOCK, axis=0) - bc)
    k_end = _dot(perm, k_end.astype(BF16)).astype(BF16)
    o_hi = o_pm.astype(BF16)
    o_lo = (o_pm - o_hi.astype(F32)).astype(BF16)
    o_intra = _dot(perm, o_hi) + _dot(perm, o_lo)
    vv_n = _halves(proj_sc, J_HI).astype(BF16)
    lane_head = lax.broadcasted_iota(jnp.int32, (HG_BLOCK, GROUP_W), 1) // HG_DK
    head_masks = [lane_head == h for h in range(HG_HEADS)]
    dec = jnp.exp(b_last)

    def per_head_rows(a):
        return jnp.concatenate([jnp.where(m, a, jnp.zeros_like(a)) for m in head_masks], axis=0)

    upds = []
    for n in range(nhb):
        rows = slice(n * HG_BLOCK, (n + 1) * HG_BLOCK)
        v_n = vv_n[rows, :]
        v_stk = jnp.concatenate([v_n[:, h * HG_DK:(h + 1) * HG_DK] for h in range(HG_HEADS)], axis=0)
        upds.append(_dot_tn(v_stk, per_head_rows(k_end[rows, :])))
    s_val = hgs_sc[...]
    states = []
    for n in range(nhb):
        states.append(s_val.astype(BF16))
        s_val = s_val * dec[n:n + 1, :] + upds[n]
    hgs_sc[...] = s_val
    for n in range(nhb):
        rows = slice(n * HG_BLOCK, (n + 1) * HG_BLOCK)
        out = _dot_nt(per_head_rows(qd[rows, :]), states[n])
        hgi_sc[rows, :] = jnp.concatenate(
            [out[h * HG_BLOCK:(h + 1) * HG_BLOCK, :] for h in range(HG_HEADS)], axis=1)
    o_h = o_intra + hgi_sc[...]
    o_ms = _dot((o_h * o_h).astype(BF16), ones256) * (1.0 / HG_DK)
    gate = _halves(proj_sc, J_HG)
    y_d = o_h * lax.rsqrt(o_ms + EPS) * hgn_ref[...] * (gate * jax.nn.sigmoid(gate))
    ycat_sc[:, 3 * GROUP_W:4 * GROUP_W] = group_norm(y_d, 3).astype(BF16)

    o_ref[0] = x + _dot(ycat_sc[...], wout_ref[...])


def _mixer_layer(x, sinks, nm, w_in, w_out, conv_w, w1, cexp, pq, dskip, wglu, qg, kg, lb, hgn, gn):
    bsz, seq, _ = x.shape
    tt = min(SEQ_TILE, seq)
    assert seq % tt == 0 and tt % CHUNK == 0 and tt >= WIN_CHUNKS * CHUNK
    nb = tt // S5_BLOCK
    assert nb % 8 == 0
    nat = jnp.arange(tt)
    perm = (jnp.arange(tt)[None, :] == ((nat % HG_BLOCK) * (tt // HG_BLOCK) + nat // HG_BLOCK)[:, None]).astype(BF16)
    const2 = lambda b, t: (0, 0)
    const3 = lambda b, t: (0, 0, 0)
    in_specs = [
        pl.BlockSpec(memory_space=pltpu.SMEM),
        pl.BlockSpec((1, tt, D_MODEL), lambda b, t: (b, t, 0)),
        pl.BlockSpec((1, D_MODEL), const2),
        pl.BlockSpec((D_MODEL, D_IN), const2),
        pl.BlockSpec((D_MODEL, D_MODEL), const2),
        pl.BlockSpec((CONV_W, GROUP_W), const2),
        pl.BlockSpec((S5_NG, 256, 384), const3),
        pl.BlockSpec((S5_NG, 128, 256), const3),
        pl.BlockSpec((3, S5_NG, 128), const3),
        pl.BlockSpec((1, GROUP_W), const2),
        pl.BlockSpec((GROUP_W, 2 * GROUP_W), const2),
        pl.BlockSpec((1, 256), const2),
        pl.BlockSpec((1, 128), const2),
        pl.BlockSpec((1, GROUP_W), const2),
        pl.BlockSpec((1, GROUP_W), const2),
        pl.BlockSpec((1, D_MODEL), const2),
        pl.BlockSpec((tt, tt), const2),
    ]
    scratch = [
        pltpu.VMEM((N_CHUNKS, tt, LANES), F32),
        pltpu.VMEM((8 + tt, GROUP_W), F32),
        pltpu.VMEM((WIN_CHUNKS * CHUNK + tt, LANES), BF16),
        pltpu.VMEM((WIN_CHUNKS * CHUNK + tt, LANES), BF16),
        pltpu.VMEM((HG_DK, GROUP_W), F32),
        pltpu.VMEM((tt, GROUP_W), F32),
        pltpu.VMEM((2, S5_NG, LANES), F32),
        pltpu.VMEM((nb * S5_NG, LANES), F32),
        pltpu.VMEM((nb * S5_NG, LANES), F32),
        pltpu.VMEM((nb * S5_NG, LANES), F32),
        pltpu.VMEM((S5_NG, nb, GROUP_W), F32),
        pltpu.VMEM((2, tt, LANES), F32),
        pltpu.VMEM((tt, GROUP_W), F32),
        pltpu.VMEM((tt, D_MODEL), BF16),
    ]
    return pl.pallas_call(
        functools.partial(_mixer_kernel, tt=tt),
        grid=(bsz, seq // tt),
        in_specs=in_specs,
        out_specs=pl.BlockSpec((1, tt, D_MODEL), lambda b, t: (b, t, 0)),
        out_shape=jax.ShapeDtypeStruct(x.shape, x.dtype),
        scratch_shapes=scratch,
        compiler_params=pltpu.CompilerParams(
            dimension_semantics=("parallel", "arbitrary"),
            vmem_limit_bytes=VMEM_LIMIT_BYTES),
        name="mixer_layer",
    )(sinks, x, nm, w_in, w_out, conv_w, w1, cexp, pq, dskip, wglu, qg, kg, lb, hgn, gn, perm)


def _ffn_kernel(x_ref, nf_ref, w1_ref, w2_ref, o_ref, acc_sc):
    x = x_ref[...]
    ms = jnp.mean(x * x, axis=-1, keepdims=True)
    hn = (x * lax.rsqrt(ms + EPS) * nf_ref[...]).astype(BF16)
    for j in range(D_FF // FFN_COLS):
        a = jnp.maximum(_dot(hn, w1_ref[:, j * FFN_COLS:(j + 1) * FFN_COLS]), 0.0)
        part = _dot((a * a).astype(BF16), w2_ref[j * FFN_COLS:(j + 1) * FFN_COLS, :])
        if j == 0:
            acc_sc[...] = x + part
        else:
            acc_sc[...] += part
    o_ref[...] = acc_sc[...]


def _ffn_layer(x2, nf, w1, w2):
    rows = x2.shape[0]
    tm = min(FFN_TILE, rows)
    assert rows % tm == 0
    return pl.pallas_call(
        _ffn_kernel,
        grid=(rows // tm,),
        in_specs=[
            pl.BlockSpec((tm, D_MODEL), lambda i: (i, 0)),
            pl.BlockSpec((1, D_MODEL), lambda i: (0, 0)),
            pl.BlockSpec((D_MODEL, D_FF), lambda i: (0, 0)),
            pl.BlockSpec((D_FF, D_MODEL), lambda i: (0, 0)),
        ],
        out_specs=pl.BlockSpec((tm, D_MODEL), lambda i: (i, 0)),
        out_shape=jax.ShapeDtypeStruct(x2.shape, x2.dtype),
        scratch_shapes=[pltpu.VMEM((tm, D_MODEL), F32)],
        compiler_params=pltpu.CompilerParams(
            dimension_semantics=("parallel",),
            vmem_limit_bytes=VMEM_LIMIT_BYTES),
        name="ffn_layer",
    )(x2, nf, w1, w2)


def _s5_tables(lam_re, lam_im, b_re, b_im, c_re, c_im, log_dt):
    hp = lax.Precision.HIGHEST
    lr = jnp.minimum(lam_re.astype(F32), -1e-4)
    li = lam_im.astype(F32)
    dt = jnp.exp(log_dt.astype(F32))[:, None]
    mag = jnp.exp(lr * dt)
    ar = mag * jnp.cos(li * dt)
    ai = mag * jnp.sin(li * dt)
    den = lr * lr + li * li
    zr = ((ar - 1.0) * lr + ai * li) / den
    zi = (ai * lr - (ar - 1.0) * li) / den
    bre = b_re.astype(F32)
    bim = b_im.astype(F32)
    bbr = zr[..., None] * bre - zi[..., None] * bim
    bbi = zr[..., None] * bim + zi[..., None] * bre
    d = jnp.arange(S5_BLOCK + 1, dtype=F32)[:, None, None]
    magd = jnp.exp(d * (lr * dt))
    pr = magd * jnp.cos(d * (li * dt))
    pi = magd * jnp.sin(d * (li * dt))
    cre = c_re.astype(F32)[None]
    cim = c_im.astype(F32)[None]
    car = cre * pr[:, :, None, :] - cim * pi[:, :, None, :]
    cai = cre * pi[:, :, None, :] + cim * pr[:, :, None, :]
    kern = jnp.einsum('dghp,gpj->dghj', jnp.concatenate([car, -cai], axis=3)[:S5_BLOCK],
                      jnp.concatenate([bbr, bbi], axis=1), precision=hp)
    kcat = kern.transpose(1, 3, 0, 2).reshape(S5_NG, S5_GROUP, 256)
    lane = jnp.arange(256)
    toep = jnp.stack([jnp.where(lane >= S5_GROUP * k, jnp.roll(kcat, S5_GROUP * k, axis=2), 0.0)
                      for k in range(S5_BLOCK)], axis=1)
    toep = toep.reshape(S5_NG, 256, 256)
    prk = jnp.flip(pr[:S5_BLOCK], axis=0).transpose(1, 0, 2)[:, :, None, :]
    pik = jnp.flip(pi[:S5_BLOCK], axis=0).transpose(1, 0, 2)[:, :, None, :]
    bbr_t = bbr.transpose(0, 2, 1)[:, None]
    bbi_t = bbi.transpose(0, 2, 1)[:, None]
    s_in = jnp.concatenate([prk * bbr_t - pik * bbi_t, prk * bbi_t + pik * bbr_t], axis=3)
    s_in = s_in.reshape(S5_NG, 256, 128)
    w1 = jnp.concatenate([toep, s_in], axis=2).astype(BF16)
    s_out = jnp.concatenate([car[1:], -cai[1:]], axis=3).transpose(1, 0, 2, 3)
    cexp = jnp.swapaxes(s_out.reshape(S5_NG, 256, 128), 1, 2).astype(BF16)
    p_rot = jnp.concatenate([pr[S5_BLOCK], pr[S5_BLOCK]], axis=-1)
    q_rot = jnp.concatenate([-pi[S5_BLOCK], pi[S5_BLOCK]], axis=-1)
    pq = jnp.stack([p_rot, q_rot, -q_rot])
    return w1, cexp, pq


def kernel(x, w_in, w_out, norm_mix, norm_ffn, conv_w, s5_lam_re, s5_lam_im, s5_b_re, s5_b_im, s5_c_re, s5_c_im, s5_d, s5_log_dt, s5_w_glu, attn_q_norm, attn_k_norm, attn_sinks, hg_lower_bounds, hg_out_norm, group_norm, w_ff1, w_ff2):
    bsz, seq, _ = x.shape
    depth = w_in.shape[0]
    lb_p = jax.nn.softmax(hg_lower_bounds.astype(F32), axis=0)
    lb_table = jnp.cumsum(lb_p, axis=0) - lb_p[0]
    def swap_mid_heads(a, base, axis):
        cuts = [0, base + ATT_HD, base + 2 * ATT_HD, base + 3 * ATT_HD, a.shape[axis]]
        parts = [lax.slice_in_dim(a, cuts[i], cuts[i + 1], axis=axis) for i in range(4)]
        return jnp.concatenate([parts[0], parts[2], parts[1], parts[3]], axis=axis)

    w1, cexp, pq = jax.vmap(_s5_tables)(s5_lam_re, s5_lam_im, s5_b_re, s5_b_im, s5_c_re, s5_c_im,
                                         s5_log_dt)
    w_in_b = swap_mid_heads(w_in, J_AQ * LANES, 2).astype(BF16)
    w_out_b = swap_mid_heads(w_out, 2 * GROUP_W, 1).astype(BF16)
    gn_b = swap_mid_heads(group_norm, 2 * GROUP_W, 1)
    w_glu_b = s5_w_glu.astype(BF16)
    w_ff1_b = w_ff1.astype(BF16)
    w_ff2_b = w_ff2.astype(BF16)
    qg = jnp.tile(attn_q_norm, (1, ATT_QH))
    kg = jnp.tile(attn_k_norm, (1, ATT_KVH))
    hgn = jnp.tile(hg_out_norm, (1, HG_HEADS))
    row = lambda a, l: a[l].reshape(1, -1)
    for l in range(depth):
        x = _mixer_layer(
            x, attn_sinks[l].astype(F32), row(norm_mix, l), w_in_b[l], w_out_b[l],
            conv_w[l].astype(F32), w1[l], cexp[l], pq[l], row(s5_d, l), w_glu_b[l],
            row(qg, l), row(kg, l), row(lb_table, l), row(hgn, l), row(gn_b, l))
        x = _ffn_layer(x.reshape(bsz * seq, D_MODEL), row(norm_ffn, l),
                       w_ff1_b[l], w_ff2_b[l]).reshape(bsz, seq, D_MODEL)
    return x
```

```python
import functools

import jax
import jax.numpy as jnp
from jax import lax
from jax.experimental import pallas as pl
from jax.experimental.pallas import tpu as pltpu

F32 = jnp.float32
BF16 = jnp.bfloat16

D_MODEL = 1024
GROUP_W = 256
N_GROUPS = 4
CHUNK = 64
CONV_W = 3
S5_GROUP = 16
S5_NG = 16
S5_STATE = 64
S5_BLOCK = 16
ATT_HD = 64
ATT_QH = 4
ATT_KVH = 2
ATT_REP = 2
ATT_SCALE = ATT_HD ** -0.5
WIN_CHUNKS = 2
HG_HEADS = 4
HG_DK = 64
HG_BLOCK = 16
D_FF = 4096
D_IN = 2560
EPS = 1e-6
LANES = 128

J_CVH, J_CVB, J_CVC, J_S5U = 0, 2, 4, 6
J_AQ, J_AK, J_AV = 8, 10, 11
J_HQ, J_HF, J_HI, J_HG = 12, 14, 16, 18
N_CHUNKS = D_IN // LANES

SEQ_TILE = 512
FFN_TILE = 1024
FFN_COLS = 1024
VMEM_LIMIT_BYTES = 56 * 1024 * 1024


def _dot(a, b):
    return jnp.dot(a, b, preferred_element_type=F32)


def _dot_nt(a, b):
    return lax.dot_general(a, b, (((1,), (1,)), ((), ())), preferred_element_type=F32)


def _dot_tn(a, b):
    return lax.dot_general(a, b, (((0,), (0,)), ((), ())), preferred_element_type=F32)


def _segment_ones(n, seg):
    r = lax.broadcasted_iota(jnp.int32, (n, n), 0) // seg
    c = lax.broadcasted_iota(jnp.int32, (n, n), 1) // seg
    return (r == c).astype(BF16)


def _chunk_transpose16(arrs):
    rows = arrs[0].shape[0]
    chunk = lax.broadcasted_iota(jnp.int32, (rows, 256), 1) // 16
    for j in range(4):
        s = 1 << j
        low = ((chunk >> j) & 1) == 0
        new = list(arrs)
        for i in range(16):
            if i & s:
                continue
            a, b = arrs[i], arrs[i + s]
            new[i] = jnp.where(low, a, pltpu.roll(b, 16 * s, axis=1))
            new[i + s] = jnp.where(low, pltpu.roll(a, 256 - 16 * s, axis=1), b)
        arrs = new
    return arrs


def _halves(ref, base):
    return jnp.concatenate([ref[base], ref[base + 1]], axis=1)


def _mixer_kernel(sinks_ref, x_ref, nm_ref, win_ref, wout_ref, convw_ref, w1_ref, cexp_ref,
                  pq_ref, dskip_ref, wglu_ref, qg_ref, kg_ref, lb_ref, hgn_ref, gn_ref, perm_ref,
                  o_ref,
                  proj_sc, zp_sc, kext_sc, vext_sc, hgs_sc, hgi_sc,
                  s5s_sc, s5c_sc, s5w_sc, s5i_sc, s5y_sc, s5o_sc, att_sc, ycat_sc,
                  *, tt):
    t = pl.program_id(1)
    nb = tt // S5_BLOCK
    nhb = tt // HG_BLOCK
    ncp = tt // CHUNK
    band = (WIN_CHUNKS + 1) * CHUNK
    hist = WIN_CHUNKS * CHUNK

    @pl.when(t == 0)
    def _reset():
        zp_sc[0:8, :] = jnp.zeros((8, GROUP_W), F32)
        kext_sc[0:hist, :] = jnp.zeros((hist, LANES), BF16)
        vext_sc[0:hist, :] = jnp.zeros((hist, LANES), BF16)
        hgs_sc[...] = jnp.zeros((HG_DK, GROUP_W), F32)
        s5s_sc[...] = jnp.zeros((2, S5_NG, LANES), F32)

    x = x_ref[0]
    ms = jnp.mean(x * x, axis=-1, keepdims=True)
    hn = (x * lax.rsqrt(ms + EPS) * nm_ref[...]).astype(BF16)

    half_rows = tt // 2
    pieces = [(c0, r0) for c0 in (768, 1536, 1792, 2048, 2304, 1024, 1280, 0, 256, 512)
              for r0 in (0, half_rows)]
    n_s5, n_hg = 2, 10
    emitted = [0]

    def emit(n=1):
        for _ in range(n):
            if emitted[0] < len(pieces):
                c0, r0 = pieces[emitted[0]]
                emitted[0] += 1
                res = _dot(hn[r0:r0 + half_rows, :], win_ref[:, c0:c0 + 256])
                for j in range(2):
                    proj_sc[c0 // LANES + j, r0:r0 + half_rows, :] = res[:, j * LANES:(j + 1) * LANES]

    def emit_until(k):
        emit(k - emitted[0])

    emit_until(n_s5)

    def group_norm(y, g):
        gain = gn_ref[:, g * GROUP_W:(g + 1) * GROUP_W]
        msq = jnp.mean(y * y, axis=-1, keepdims=True)
        return y * lax.rsqrt(msq + EPS) * gain

    ones256 = _segment_ones(256, 64)
    ones128 = _segment_ones(128, 64)

    folded = [jnp.concatenate([proj_sc[J_S5U + half, pl.ds(k, nb, stride=S5_BLOCK), :]
                               for half in range(2)], axis=1)
              for k in range(S5_BLOCK)]
    emit()
    per_group = _chunk_transpose16(folded)
    emit()
    for g in range(S5_NG):
        zg = _dot(per_group[g].astype(BF16), w1_ref[g])
        s5y_sc[g] = zg[:, 0:GROUP_W]
        s5c_sc[pl.ds(g, nb, stride=S5_NG), :] = zg[:, GROUP_W:GROUP_W + LANES]
        if g % 8 == 7:
            emit()
    s5w_sc[...] = pltpu.roll(s5c_sc[...], 64, axis=1)
    p_rot, q_rot, q_swp = pq_ref[0], pq_ref[1], pq_ref[2]
    st, sw = s5s_sc[0], s5s_sc[1]
    for b in range(nb):
        s5i_sc[b * S5_NG:(b + 1) * S5_NG, :] = st
        c_b = s5c_sc[b * S5_NG:(b + 1) * S5_NG, :]
        w_b = s5w_sc[b * S5_NG:(b + 1) * S5_NG, :]
        st, sw = (p_rot * st + q_rot * sw + c_b, p_rot * sw + q_swp * st + w_b)
    s5s_sc[0] = st
    s5s_sc[1] = sw
    emit()
    outs = []
    for g in range(S5_NG):
        s_in = s5i_sc[pl.ds(g, nb, stride=S5_NG), :]
        outs.append(s5y_sc[g] + _dot(s_in.astype(BF16), cexp_ref[g]))
    emit()
    unfolded = _chunk_transpose16(outs)
    emit()
    for k in range(S5_BLOCK):
        for half in range(2):
            s5o_sc[half, pl.ds(k, nb, stride=S5_BLOCK), :] = (
                unfolded[k][:, LANES * half:LANES * (half + 1)])
    y5 = _halves(s5o_sc, 0) + dskip_ref[...] * _halves(proj_sc, J_S5U)
    y5 = jax.nn.gelu(y5)
    emit_until(n_hg)
    glu = _dot(y5.astype(BF16), wglu_ref[...])
    y_b = glu[:, 0:GROUP_W] * jax.nn.sigmoid(glu[:, GROUP_W:2 * GROUP_W])
    ycat_sc[:, GROUP_W:2 * GROUP_W] = group_norm(y_b, 1).astype(BF16)

    def pos_major(j):
        return jnp.concatenate(
            [jnp.concatenate([proj_sc[j + half, pl.ds(p, nhb, stride=HG_BLOCK), :]
                              for p in range(HG_BLOCK)], axis=0)
             for half in range(2)], axis=1)

    hq = pos_major(J_HQ)
    vv = pos_major(J_HI)
    lb = lb_ref[...]
    fg = lb + (1.0 - lb) * jax.nn.sigmoid(pos_major(J_HF))
    kk = 1.0 - fg
    lf = jnp.log(fg)
    cum = [lf[0:nhb, :]]
    for p in range(1, HG_BLOCK):
        cum.append(cum[-1] + lf[p * nhb:(p + 1) * nhb, :])
    bc = jnp.concatenate(cum, axis=0)
    b_last = cum[-1]
    gk = jnp.log(kk) - bc
    o_groups = []
    for p in range(HG_BLOCK):
        m = (p + 1) * nhb
        q_p = hq[p * nhb:m, :][None]
        b_p = bc[p * nhb:m, :][None]
        e = q_p * jnp.exp(b_p + gk[0:m, :].reshape(p + 1, nhb, GROUP_W))
        sc = _dot(e.reshape(m, GROUP_W).astype(BF16), ones256) * vv[0:m, :]
        o_groups.append(jnp.sum(sc.reshape(p + 1, nhb, GROUP_W), axis=0))
        if p >= 6:
            emit()
    o_pm = jnp.concatenate(o_groups, axis=0)
    emit_until(len(pieces))
    perm = perm_ref[...]
    qd = _dot(perm, (hq * jnp.exp(bc)).astype(BF16)).astype(BF16)
    k_end = kk * jnp.exp(jnp.concatenate([b_last] * HG_BLOCK, axis=0) - bc)
    k_end = _dot(perm, k_end.astype(BF16)).astype(BF16)
    o_hi = o_pm.astype(BF16)
    o_lo = (o_pm - o_hi.astype(F32)).astype(BF16)
    o_intra = _dot(perm, o_hi) + _dot(perm, o_lo)
    vv_n = _halves(proj_sc, J_HI).astype(BF16)
    lane_head = lax.broadcasted_iota(jnp.int32, (HG_BLOCK, GROUP_W), 1) // HG_DK
    head_masks = [lane_head == h for h in range(HG_HEADS)]
    dec = jnp.exp(b_last)

    def per_head_rows(a):
        return jnp.concatenate([jnp.where(m, a, jnp.zeros_like(a)) for m in head_masks], axis=0)

    upds = []
    for n in range(nhb):
        rows = slice(n * HG_BLOCK, (n + 1) * HG_BLOCK)
        v_n = vv_n[rows, :]
        v_stk = jnp.concatenate([v_n[:, h * HG_DK:(h + 1) * HG_DK] for h in range(HG_HEADS)], axis=0)
        upds.append(_dot_tn(v_stk, per_head_rows(k_end[rows, :])))
    s_val = hgs_sc[...]
    states = []
    for n in range(nhb):
        states.append(s_val.astype(BF16))
        s_val = s_val * dec[n:n + 1, :] + upds[n]
    hgs_sc[...] = s_val
    for n in range(nhb):
        rows = slice(n * HG_BLOCK, (n + 1) * HG_BLOCK)
        out = _dot_nt(per_head_rows(qd[rows, :]), states[n])
        hgi_sc[rows, :] = jnp.concatenate(
            [out[h * HG_BLOCK:(h + 1) * HG_BLOCK, :] for h in range(HG_HEADS)], axis=1)
    o_h = o_intra + hgi_sc[...]
    o_ms = _dot((o_h * o_h).astype(BF16), ones256) * (1.0 / HG_DK)
    gate = _halves(proj_sc, J_HG)
    y_d = o_h * lax.rsqrt(o_ms + EPS) * hgn_ref[...] * (gate * jax.nn.sigmoid(gate))
    ycat_sc[:, 3 * GROUP_W:4 * GROUP_W] = group_norm(y_d, 3).astype(BF16)

    z = _halves(proj_sc, J_CVC) * _halves(proj_sc, J_CVH)
    zp_sc[8:8 + tt, :] = z
    acc = (convw_ref[0:1, :] * zp_sc[6:6 + tt, :] + convw_ref[1:2, :] * zp_sc[7:7 + tt, :]
           + convw_ref[2:3, :] * z)
    y_a = _halves(proj_sc, J_CVB) * acc
    zp_sc[0:8, :] = zp_sc[tt:tt + 8, :]
    ycat_sc[:, 0:GROUP_W] = group_norm(y_a, 0).astype(BF16)

    aq = _halves(proj_sc, J_AQ)
    ak = proj_sc[J_AK]
    q_ms = _dot((aq * aq).astype(BF16), ones256) * (1.0 / ATT_HD)
    k_ms = _dot((ak * ak).astype(BF16), ones128) * (1.0 / ATT_HD)
    qn = aq * lax.rsqrt(q_ms + EPS) * qg_ref[...] * ATT_SCALE
    kn = ak * lax.rsqrt(k_ms + EPS) * kg_ref[...]
    kext_sc[hist:hist + tt, :] = kn.astype(BF16)
    vext_sc[hist:hist + tt, :] = proj_sc[J_AV].astype(BF16)
    lane_kv = lax.broadcasted_iota(jnp.int32, (tt, LANES), 1) // ATT_HD
    q_masked = [[jnp.where(lane_kv == g, qn[:, r * LANES:(r + 1) * LANES], 0.0).astype(BF16)
                 for r in range(ATT_REP)] for g in range(ATT_KVH)]
    col = lax.broadcasted_iota(jnp.int32, (ATT_REP * CHUNK, band), 1)
    row = lax.broadcasted_iota(jnp.int32, (ATT_REP * CHUNK, 1), 0)
    first_kv = lax.broadcasted_iota(jnp.int32, (CHUNK, LANES), 1) < ATT_HD
    blocks = [(c, g) for c in range(ncp) for g in range(ATT_KVH)]
    scores = []
    for c, g in blocks:
        q2 = jnp.concatenate([q_masked[g][r][c * CHUNK:(c + 1) * CHUNK, :] for r in range(ATT_REP)],
                             axis=0)
        scores.append(_dot_nt(q2, kext_sc[c * CHUNK:c * CHUNK + band, :]))
    probs, dens = [], []
    for (c, g), s in zip(blocks, scores):
        if c < WIN_CHUNKS:
            first_valid = jnp.maximum(WIN_CHUNKS - (t * ncp + c), 0) * CHUNK
            s = jnp.where(col >= first_valid, s, -jnp.inf)
        sink = jnp.where(row < CHUNK, sinks_ref[ATT_REP * g], sinks_ref[ATT_REP * g + 1])
        m = jnp.maximum(jnp.max(s, axis=-1, keepdims=True), sink)
        p = jnp.exp(s - m)
        dens.append(jnp.sum(p, axis=-1, keepdims=True) + jnp.exp(sink - m))
        probs.append(p.astype(BF16))
    outs_kv = {}
    for (c, g), p, den in zip(blocks, probs, dens):
        outs_kv[c, g] = _dot(p, vext_sc[c * CHUNK:c * CHUNK + band, :]) / den
    for c in range(ncp):
        for r in range(ATT_REP):
            att_sc[c * CHUNK:(c + 1) * CHUNK, r * LANES:(r + 1) * LANES] = jnp.where(
                first_kv, outs_kv[c, 0][r * CHUNK:(r + 1) * CHUNK, :],
                outs_kv[c, 1][r * CHUNK:(r + 1) * CHUNK, :])
    kext_sc[0:hist, :] = kext_sc[tt:tt + hist, :]
    vext_sc[0:hist, :] = vext_sc[tt:tt + hist, :]
    ycat_sc[:, 2 * GROUP_W:3 * GROUP_W] = group_norm(att_sc[...], 2).astype(BF16)

    o_ref[0] = x + _dot(ycat_sc[...], wout_ref[...])


def _mixer_layer(x, sinks, nm, w_in, w_out, conv_w, w1, cexp, pq, dskip, wglu, qg, kg, lb, hgn, gn):
    bsz, seq, _ = x.shape
    tt = min(SEQ_TILE, seq)
    assert seq % tt == 0 and tt % CHUNK == 0 and tt >= WIN_CHUNKS * CHUNK
    nb = tt // S5_BLOCK
    assert nb % 8 == 0
    nat = jnp.arange(tt)
    perm = (jnp.arange(tt)[None, :] == ((nat % HG_BLOCK) * (tt // HG_BLOCK) + nat // HG_BLOCK)[:, None]).astype(BF16)
    const2 = lambda b, t: (0, 0)
    const3 = lambda b, t: (0, 0, 0)
    in_specs = [
        pl.BlockSpec(memory_space=pltpu.SMEM),
        pl.BlockSpec((1, tt, D_MODEL), lambda b, t: (b, t, 0)),
        pl.BlockSpec((1, D_MODEL), const2),
        pl.BlockSpec((D_MODEL, D_IN), const2),
        pl.BlockSpec((D_MODEL, D_MODEL), const2),
        pl.BlockSpec((CONV_W, GROUP_W), const2),
        pl.BlockSpec((S5_NG, 256, 384), const3),
        pl.BlockSpec((S5_NG, 128, 256), const3),
        pl.BlockSpec((3, S5_NG, 128), const3),
        pl.BlockSpec((1, GROUP_W), const2),
        pl.BlockSpec((GROUP_W, 2 * GROUP_W), const2),
        pl.BlockSpec((1, 256), const2),
        pl.BlockSpec((1, 128), const2),
        pl.BlockSpec((1, GROUP_W), const2),
        pl.BlockSpec((1, GROUP_W), const2),
        pl.BlockSpec((1, D_MODEL), const2),
        pl.BlockSpec((tt, tt), const2),
    ]
    scratch = [
        pltpu.VMEM((N_CHUNKS, tt, LANES), F32),
        pltpu.VMEM((8 + tt, GROUP_W), F32),
        pltpu.VMEM((WIN_CHUNKS * CHUNK + tt, LANES), BF16),
        pltpu.VMEM((WIN_CHUNKS * CHUNK + tt, LANES), BF16),
        pltpu.VMEM((HG_DK, GROUP_W), F32),
        pltpu.VMEM((tt, GROUP_W), F32),
        pltpu.VMEM((2, S5_NG, LANES), F32),
        pltpu.VMEM((nb * S5_NG, LANES), F32),
        pltpu.VMEM((nb * S5_NG, LANES), F32),
        pltpu.VMEM((nb * S5_NG, LANES), F32),
        pltpu.VMEM((S5_NG, nb, GROUP_W), F32),
        pltpu.VMEM((2, tt, LANES), F32),
        pltpu.VMEM((tt, GROUP_W), F32),
        pltpu.VMEM((tt, D_MODEL), BF16),
    ]
    return pl.pallas_call(
        functools.partial(_mixer_kernel, tt=tt),
        grid=(bsz, seq // tt),
        in_specs=in_specs,
        out_specs=pl.BlockSpec((1, tt, D_MODEL), lambda b, t: (b, t, 0)),
        out_shape=jax.ShapeDtypeStruct(x.shape, x.dtype),
        scratch_shapes=scratch,
        compiler_params=pltpu.CompilerParams(
            dimension_semantics=("parallel", "arbitrary"),
            vmem_limit_bytes=VMEM_LIMIT_BYTES),
        name="mixer_layer",
    )(sinks, x, nm, w_in, w_out, conv_w, w1, cexp, pq, dskip, wglu, qg, kg, lb, hgn, gn, perm)


def _ffn_kernel(x_ref, nf_ref, w1_ref, w2_ref, o_ref, acc_sc):
    x = x_ref[...]
    ms = jnp.mean(x * x, axis=-1, keepdims=True)
    hn = (x * lax.rsqrt(ms + EPS) * nf_ref[...]).astype(BF16)
    for j in range(D_FF // FFN_COLS):
        a = jnp.maximum(_dot(hn, w1_ref[:, j * FFN_COLS:(j + 1) * FFN_COLS]), 0.0)
        part = _dot((a * a).astype(BF16), w2_ref[j * FFN_COLS:(j + 1) * FFN_COLS, :])
        if j == 0:
            acc_sc[...] = x + part
        else:
            acc_sc[...] += part
    o_ref[...] = acc_sc[...]


def _ffn_layer(x2, nf, w1, w2):
    rows = x2.shape[0]
    tm = min(FFN_TILE, rows)
    assert rows % tm == 0
    return pl.pallas_call(
        _ffn_kernel,
        grid=(rows // tm,),
        in_specs=[
            pl.BlockSpec((tm, D_MODEL), lambda i: (i, 0)),
            pl.BlockSpec((1, D_MODEL), lambda i: (0, 0)),
            pl.BlockSpec((D_MODEL, D_FF), lambda i: (0, 0), pipeline_mode=pl.Buffered(1)),
            pl.BlockSpec((D_FF, D_MODEL), lambda i: (0, 0), pipeline_mode=pl.Buffered(1)),
        ],
        out_specs=pl.BlockSpec((tm, D_MODEL), lambda i: (i, 0)),
        out_shape=jax.ShapeDtypeStruct(x2.shape, x2.dtype),
        scratch_shapes=[pltpu.VMEM((tm, D_MODEL), F32)],
        compiler_params=pltpu.CompilerParams(
            dimension_semantics=("parallel",),
            vmem_limit_bytes=VMEM_LIMIT_BYTES),
        name="ffn_layer",
    )(x2, nf, w1, w2)


def _s5_tables(lam_re, lam_im, b_re, b_im, c_re, c_im, log_dt):
    hp = lax.Precision.HIGHEST
    lr = jnp.minimum(lam_re.astype(F32), -1e-4)
    li = lam_im.astype(F32)
    dt = jnp.exp(log_dt.astype(F32))[:, None]
    mag = jnp.exp(lr * dt)
    ar = mag * jnp.cos(li * dt)
    ai = mag * jnp.sin(li * dt)
    den = lr * lr + li * li
    zr = ((ar - 1.0) * lr + ai * li) / den
    zi = (ai * lr - (ar - 1.0) * li) / den
    bre = b_re.astype(F32)
    bim = b_im.astype(F32)
    bbr = zr[..., None] * bre - zi[..., None] * bim
    bbi = zr[..., None] * bim + zi[..., None] * bre
    d = jnp.arange(S5_BLOCK + 1, dtype=F32)[:, None, None]
    magd = jnp.exp(d * (lr * dt))
    pr = magd * jnp.cos(d * (li * dt))
    pi = magd * jnp.sin(d * (li * dt))
    cre = c_re.astype(F32)[None]
    cim = c_im.astype(F32)[None]
    car = cre * pr[:, :, None, :] - cim * pi[:, :, None, :]
    cai = cre * pi[:, :, None, :] + cim * pr[:, :, None, :]
    c_a = jnp.concatenate([car, -cai], axis=3)[:S5_BLOCK].transpose(1, 0, 2, 3)
    kern = jnp.einsum('gmp,gpj->gmj', c_a.reshape(S5_NG, 256, 2 * S5_STATE),
                      jnp.concatenate([bbr, bbi], axis=1), precision=hp)
    kcat = jnp.swapaxes(kern, 1, 2)
    lane = jnp.arange(256)
    toep = jnp.stack([jnp.where(lane >= S5_GROUP * k, jnp.roll(kcat, S5_GROUP * k, axis=2), 0.0)
                      for k in range(S5_BLOCK)], axis=1)
    toep = toep.reshape(S5_NG, 256, 256)
    prk = jnp.flip(pr[:S5_BLOCK], axis=0).transpose(1, 0, 2)[:, :, None, :]
    pik = jnp.flip(pi[:S5_BLOCK], axis=0).transpose(1, 0, 2)[:, :, None, :]
    bbr_t = bbr.transpose(0, 2, 1)[:, None]
    bbi_t = bbi.transpose(0, 2, 1)[:, None]
    s_in = jnp.concatenate([prk * bbr_t - pik * bbi_t, prk * bbi_t + pik * bbr_t], axis=3)
    s_in = s_in.reshape(S5_NG, 256, 128)
    w1 = jnp.concatenate([toep, s_in], axis=2).astype(BF16)
    s_out = jnp.concatenate([car[1:], -cai[1:]], axis=3).transpose(1, 0, 2, 3)
    cexp = jnp.swapaxes(s_out.reshape(S5_NG, 256, 128), 1, 2).astype(BF16)
    p_rot = jnp.concatenate([pr[S5_BLOCK], pr[S5_BLOCK]], axis=-1)
    q_rot = jnp.concatenate([-pi[S5_BLOCK], pi[S5_BLOCK]], axis=-1)
    pq = jnp.stack([p_rot, q_rot, -q_rot])
    return w1, cexp, pq


def kernel(x, w_in, w_out, norm_mix, norm_ffn, conv_w, s5_lam_re, s5_lam_im, s5_b_re, s5_b_im, s5_c_re, s5_c_im, s5_d, s5_log_dt, s5_w_glu, attn_q_norm, attn_k_norm, attn_sinks, hg_lower_bounds, hg_out_norm, group_norm, w_ff1, w_ff2):
    bsz, seq, _ = x.shape
    depth = w_in.shape[0]
    lb_p = jax.nn.softmax(hg_lower_bounds.astype(F32), axis=0)
    lb_table = jnp.cumsum(lb_p, axis=0) - lb_p[0]
    def swap_mid_heads(a, base, axis):
        cuts = [0, base + ATT_HD, base + 2 * ATT_HD, base + 3 * ATT_HD, a.shape[axis]]
        parts = [lax.slice_in_dim(a, cuts[i], cuts[i + 1], axis=axis) for i in range(4)]
        return jnp.concatenate([parts[0], parts[2], parts[1], parts[3]], axis=axis)

    w1, cexp, pq = jax.vmap(_s5_tables)(s5_lam_re, s5_lam_im, s5_b_re, s5_b_im, s5_c_re, s5_c_im,
                                         s5_log_dt)
    w_in_b = swap_mid_heads(w_in, J_AQ * LANES, 2).astype(BF16)
    w_out_b = swap_mid_heads(w_out, 2 * GROUP_W, 1).astype(BF16)
    gn_b = swap_mid_heads(group_norm, 2 * GROUP_W, 1)
    w_glu_b = s5_w_glu.astype(BF16)
    w_ff1_b = w_ff1.astype(BF16)
    w_ff2_b = w_ff2.astype(BF16)
    qg = jnp.tile(attn_q_norm, (1, ATT_QH))
    kg = jnp.tile(attn_k_norm, (1, ATT_KVH))
    hgn = jnp.tile(hg_out_norm, (1, HG_HEADS))
    row = lambda a, l: a[l].reshape(1, -1)
    for l in range(depth):
        x = _mixer_layer(
            x, attn_sinks[l].astype(F32), row(norm_mix, l), w_in_b[l], w_out_b[l],
            conv_w[l].astype(F32), w1[l], cexp[l], pq[l], row(s5_d, l), w_glu_b[l],
            row(qg, l), row(kg, l), row(lb_table, l), row(hgn, l), row(gn_b, l))
        x = _ffn_layer(x.reshape(bsz * seq, D_MODEL), row(norm_ffn, l),
                       w_ff1_b[l], w_ff2_b[l]).reshape(bsz, seq, D_MODEL)
    return x
```

```python
import functools

import jax
import jax.numpy as jnp
from jax import lax
from jax.experimental import pallas as pl
from jax.experimental.pallas import tpu as pltpu

F32 = jnp.float32
BF16 = jnp.bfloat16

D_MODEL = 1024
GROUP_W = 256
N_GROUPS = 4
CHUNK = 64
CONV_W = 3
S5_GROUP = 16
S5_NG = 16
S5_STATE = 64
S5_BLOCK = 16
ATT_HD = 64
ATT_QH = 4
ATT_KVH = 2
ATT_REP = 2
ATT_SCALE = ATT_HD ** -0.5
WIN_CHUNKS = 2
HG_HEADS = 4
HG_DK = 64
HG_BLOCK = 16
D_FF = 4096
D_IN = 2560
EPS = 1e-6
LANES = 128

J_CVH, J_CVB, J_CVC, J_S5U = 0, 2, 4, 6
J_AQ, J_AK, J_AV = 8, 10, 11
J_HQ, J_HF, J_HI, J_HG = 12, 14, 16, 18
N_CHUNKS = D_IN // LANES

SEQ_TILE = 512
FFN_TILE = 1024
FFN_COLS = 1024
VMEM_LIMIT_BYTES = 56 * 1024 * 1024


def _dot(a, b):
    return jnp.dot(a, b, preferred_element_type=F32)


def _dot_nt(a, b):
    return lax.dot_general(a, b, (((1,), (1,)), ((), ())), preferred_element_type=F32)


def _dot_tn(a, b):
    return lax.dot_general(a, b, (((0,), (0,)), ((), ())), preferred_element_type=F32)


def _segment_ones(n, seg):
    r = lax.broadcasted_iota(jnp.int32, (n, n), 0) // seg
    c = lax.broadcasted_iota(jnp.int32, (n, n), 1) // seg
    return (r == c).astype(BF16)


def _chunk_transpose16(arrs):
    rows = arrs[0].shape[0]
    chunk = lax.broadcasted_iota(jnp.int32, (rows, 256), 1) // 16
    for j in range(4):
        s = 1 << j
        low = ((chunk >> j) & 1) == 0
        new = list(arrs)
        for i in range(16):
            if i & s:
                continue
            a, b = arrs[i], arrs[i + s]
            new[i] = jnp.where(low, a, pltpu.roll(b, 16 * s, axis=1))
            new[i + s] = jnp.where(low, pltpu.roll(a, 256 - 16 * s, axis=1), b)
        arrs = new
    return arrs


def _halves(ref, base):
    return jnp.concatenate([ref[base], ref[base + 1]], axis=1)


def _mixer_kernel(sinks_ref, x_ref, nm_ref, win_ref, wout_ref, convw_ref, w1_ref, cexp_ref,
                  pq_ref, dskip_ref, wglu_ref, qg_ref, kg_ref, lb_ref, hgn_ref, gn_ref, perm_ref,
                  o_ref,
                  proj_sc, zp_sc, kext_sc, vext_sc, hgs_sc, hgi_sc,
                  s5s_sc, s5c_sc, s5w_sc, s5i_sc, s5y_sc, s5o_sc, att_sc, ycat_sc,
                  *, tt):
    t = pl.program_id(1)
    nb = tt // S5_BLOCK
    nhb = tt // HG_BLOCK
    ncp = tt // CHUNK
    band = (WIN_CHUNKS + 1) * CHUNK
    hist = WIN_CHUNKS * CHUNK

    @pl.when(t == 0)
    def _reset():
        zp_sc[0:8, :] = jnp.zeros((8, GROUP_W), F32)
        kext_sc[0:hist, :] = jnp.zeros((hist, LANES), BF16)
        vext_sc[0:hist, :] = jnp.zeros((hist, LANES), BF16)
        hgs_sc[...] = jnp.zeros((HG_DK, GROUP_W), F32)
        s5s_sc[...] = jnp.zeros((2, S5_NG, LANES), F32)

    x = x_ref[0]
    ms = jnp.mean(x * x, axis=-1, keepdims=True)
    hn = (x * lax.rsqrt(ms + EPS) * nm_ref[...]).astype(BF16)

    half_rows = tt // 2
    pieces = [(c0, r0) for c0 in (768, 1536, 1792, 2048, 2304, 1024, 1280, 0, 256, 512)
              for r0 in (0, half_rows)]
    n_s5, n_hg = 2, 10
    emitted = [0]

    def emit(n=1):
        for _ in range(n):
            if emitted[0] < len(pieces):
                c0, r0 = pieces[emitted[0]]
                emitted[0] += 1
                res = _dot(hn[r0:r0 + half_rows, :], win_ref[:, c0:c0 + 256])
                for j in range(2):
                    proj_sc[c0 // LANES + j, r0:r0 + half_rows, :] = res[:, j * LANES:(j + 1) * LANES]

    def emit_until(k):
        emit(k - emitted[0])

    emit_until(n_s5)

    def group_norm(y, g):
        gain = gn_ref[:, g * GROUP_W:(g + 1) * GROUP_W]
        msq = jnp.mean(y * y, axis=-1, keepdims=True)
        return y * lax.rsqrt(msq + EPS) * gain

    ones256 = _segment_ones(256, 64)
    ones128 = _segment_ones(128, 64)

    folded = [jnp.concatenate([proj_sc[J_S5U + half, pl.ds(k, nb, stride=S5_BLOCK), :]
                               for half in range(2)], axis=1)
              for k in range(S5_BLOCK)]
    emit()
    per_group = _chunk_transpose16(folded)
    emit()
    for g in range(S5_NG):
        zg = _dot(per_group[g].astype(BF16), w1_ref[g])
        s5y_sc[g] = zg[:, 0:GROUP_W]
        s5c_sc[pl.ds(g, nb, stride=S5_NG), :] = zg[:, GROUP_W:GROUP_W + LANES]
        if g % 8 == 7:
            emit()
    s5w_sc[...] = pltpu.roll(s5c_sc[...], 64, axis=1)
    p_rot, q_rot, q_swp = pq_ref[0], pq_ref[1], pq_ref[2]
    st, sw = s5s_sc[0], s5s_sc[1]
    for b in range(nb):
        s5i_sc[b * S5_NG:(b + 1) * S5_NG, :] = st
        c_b = s5c_sc[b * S5_NG:(b + 1) * S5_NG, :]
        w_b = s5w_sc[b * S5_NG:(b + 1) * S5_NG, :]
        st, sw = (p_rot * st + q_rot * sw + c_b, p_rot * sw + q_swp * st + w_b)
    s5s_sc[0] = st
    s5s_sc[1] = sw
    emit()
    outs = []
    for g in range(S5_NG):
        s_in = s5i_sc[pl.ds(g, nb, stride=S5_NG), :]
        outs.append(s5y_sc[g] + _dot(s_in.astype(BF16), cexp_ref[g]))
    emit()
    unfolded = _chunk_transpose16(outs)
    emit()
    for k in range(S5_BLOCK):
        for half in range(2):
            s5o_sc[half, pl.ds(k, nb, stride=S5_BLOCK), :] = (
                unfolded[k][:, LANES * half:LANES * (half + 1)])
    y5 = _halves(s5o_sc, 0) + dskip_ref[...] * _halves(proj_sc, J_S5U)
    y5 = jax.nn.gelu(y5)
    emit_until(n_hg)
    glu = _dot(y5.astype(BF16), wglu_ref[...])
    y_b = glu[:, 0:GROUP_W] * jax.nn.sigmoid(glu[:, GROUP_W:2 * GROUP_W])
    ycat_sc[:, GROUP_W:2 * GROUP_W] = group_norm(y_b, 1).astype(BF16)

    def pos_major(j):
        return jnp.concatenate(
            [jnp.concatenate([proj_sc[j + half, pl.ds(p, nhb, stride=HG_BLOCK), :]
                              for p in range(HG_BLOCK)], axis=0)
             for half in range(2)], axis=1)

    hq = pos_major(J_HQ)
    vv = pos_major(J_HI)
    lb = lb_ref[...]
    fg = lb + (1.0 - lb) * jax.nn.sigmoid(pos_major(J_HF))
    kk = 1.0 - fg
    lf = jnp.log(fg)
    cum = [lf[0:nhb, :]]
    for p in range(1, HG_BLOCK):
        cum.append(cum[-1] + lf[p * nhb:(p + 1) * nhb, :])
    bc = jnp.concatenate(cum, axis=0)
    b_last = cum[-1]
    gk = jnp.log(kk) - bc
    o_groups = []
    for p in range(HG_BLOCK):
        m = (p + 1) * nhb
        q_p = hq[p * nhb:m, :][None]
        b_p = bc[p * nhb:m, :][None]
        e = q_p * jnp.exp(b_p + gk[0:m, :].reshape(p + 1, nhb, GROUP_W))
        sc = _dot(e.reshape(m, GROUP_W).astype(BF16), ones256) * vv[0:m, :]
        o_groups.append(jnp.sum(sc.reshape(p + 1, nhb, GROUP_W), axis=0))
        if p >= 6:
            emit()
    o_pm = jnp.concatenate(o_groups, axis=0)
    emit_until(len(pieces))
    perm = perm_ref[...]
    qd = _dot(perm, (hq * jnp.exp(bc)).astype(BF16)).astype(BF16)
    k_end = kk * jnp.exp(jnp.concatenate([b_last] * HG_BLOCK, axis=0) - bc)
    k_end = _dot(perm, k_end.astype(BF16)).astype(BF16)
    o_hi = o_pm.astype(BF16)
    o_lo = (o_pm - o_hi.astype(F32)).astype(BF16)
    o_intra = _dot(perm, o_hi) + _dot(perm, o_lo)
    vv_n = _halves(proj_sc, J_HI).astype(BF16)
    lane_head = lax.broadcasted_iota(jnp.int32, (HG_BLOCK, GROUP_W), 1) // HG_DK
    head_masks = [lane_head == h for h in range(HG_HEADS)]
    dec = jnp.exp(b_last)

    def per_head_rows(a):
        return jnp.concatenate([jnp.where(m, a, jnp.zeros_like(a)) for m in head_masks], axis=0)

    upds = []
    for n in range(nhb):
        rows = slice(n * HG_BLOCK, (n + 1) * HG_BLOCK)
        v_n = vv_n[rows, :]
        v_stk = jnp.concatenate([v_n[:, h * HG_DK:(h + 1) * HG_DK] for h in range(HG_HEADS)], axis=0)
        upds.append(_dot_tn(v_stk, per_head_rows(k_end[rows, :])))
    s_val = hgs_sc[...]
    states = []
    for n in range(nhb):
        states.append(s_val.astype(BF16))
        s_val = s_val * dec[n:n + 1, :] + upds[n]
    hgs_sc[...] = s_val
    for n in range(nhb):
        rows = slice(n * HG_BLOCK, (n + 1) * HG_BLOCK)
        out = _dot_nt(per_head_rows(qd[rows, :]), states[n])
        hgi_sc[rows, :] = jnp.concatenate(
            [out[h * HG_BLOCK:(h + 1) * HG_BLOCK, :] for h in range(HG_HEADS)], axis=1)
    o_h = o_intra + hgi_sc[...]
    o_ms = _dot((o_h * o_h).astype(BF16), ones256) * (1.0 / HG_DK)
    gate = _halves(proj_sc, J_HG)
    y_d = o_h * lax.rsqrt(o_ms + EPS) * hgn_ref[...] * (gate * jax.nn.sigmoid(gate))
    ycat_sc[:, 3 * GROUP_W:4 * GROUP_W] = group_norm(y_d, 3).astype(BF16)

    z = _halves(proj_sc, J_CVC) * _halves(proj_sc, J_CVH)
    zp_sc[8:8 + tt, :] = z
    acc = (convw_ref[0:1, :] * zp_sc[6:6 + tt, :] + convw_ref[1:2, :] * zp_sc[7:7 + tt, :]
           + convw_ref[2:3, :] * z)
    y_a = _halves(proj_sc, J_CVB) * acc
    zp_sc[0:8, :] = zp_sc[tt:tt + 8, :]
    ycat_sc[:, 0:GROUP_W] = group_norm(y_a, 0).astype(BF16)

    aq = _halves(proj_sc, J_AQ)
    ak = proj_sc[J_AK]
    q_ms = _dot((aq * aq).astype(BF16), ones256) * (1.0 / ATT_HD)
    k_ms = _dot((ak * ak).astype(BF16), ones128) * (1.0 / ATT_HD)
    qn = aq * lax.rsqrt(q_ms + EPS) * qg_ref[...] * ATT_SCALE
    kn = ak * lax.rsqrt(k_ms + EPS) * kg_ref[...]
    kext_sc[hist:hist + tt, :] = kn.astype(BF16)
    vext_sc[hist:hist + tt, :] = proj_sc[J_AV].astype(BF16)
    lane_kv = lax.broadcasted_iota(jnp.int32, (tt, LANES), 1) // ATT_HD
    q_masked = [[jnp.where(lane_kv == g, qn[:, r * LANES:(r + 1) * LANES], 0.0).astype(BF16)
                 for r in range(ATT_REP)] for g in range(ATT_KVH)]
    col = lax.broadcasted_iota(jnp.int32, (ATT_REP * CHUNK, band), 1)
    row = lax.broadcasted_iota(jnp.int32, (ATT_REP * CHUNK, 1), 0)
    first_kv = lax.broadcasted_iota(jnp.int32, (CHUNK, LANES), 1) < ATT_HD
    blocks = [(c, g) for c in range(ncp) for g in range(ATT_KVH)]
    scores = []
    for c, g in blocks:
        q2 = jnp.concatenate([q_masked[g][r][c * CHUNK:(c + 1) * CHUNK, :] for r in range(ATT_REP)],
                             axis=0)
        scores.append(_dot_nt(q2, kext_sc[c * CHUNK:c * CHUNK + band, :]))
    probs, dens = [], []
    for (c, g), s in zip(blocks, scores):
        if c < WIN_CHUNKS:
            first_valid = jnp.maximum(WIN_CHUNKS - (t * ncp + c), 0) * CHUNK
            s = jnp.where(col >= first_valid, s, -jnp.inf)
        sink = jnp.where(row < CHUNK, sinks_ref[ATT_REP * g], sinks_ref[ATT_REP * g + 1])
        m = jnp.maximum(jnp.max(s, axis=-1, keepdims=True), sink)
        p = jnp.exp(s - m)
        dens.append(jnp.sum(p, axis=-1, keepdims=True) + jnp.exp(sink - m))
        probs.append(p.astype(BF16))
    outs_kv = {}
    for (c, g), p, den in zip(blocks, probs, dens):
        outs_kv[c, g] = _dot(p, vext_sc[c * CHUNK:c * CHUNK + band, :]) / den
    for c in range(ncp):
        for r in range(ATT_REP):
            att_sc[c * CHUNK:(c + 1) * CHUNK, r * LANES:(r + 1) * LANES] = jnp.where(
                first_kv, outs_kv[c, 0][r * CHUNK:(r + 1) * CHUNK, :],
                outs_kv[c, 1][r * CHUNK:(r + 1) * CHUNK, :])
    kext_sc[0:hist, :] = kext_sc[tt:tt + hist, :]
    vext_sc[0:hist, :] = vext_sc[tt:tt + hist, :]
    ycat_sc[:, 2 * GROUP_W:3 * GROUP_W] = group_norm(att_sc[...], 2).astype(BF16)

    o_ref[0] = x + _dot(ycat_sc[...], wout_ref[...])


def _mixer_layer(layer, x, sinks, nm, w_in, w_out, conv_w, w1, cexp, pq, dskip, wglu, qg, kg, lb, hgn, gn):
    bsz, seq, _ = x.shape
    tt = min(SEQ_TILE, seq)
    assert seq % tt == 0 and tt % CHUNK == 0 and tt >= WIN_CHUNKS * CHUNK
    nb = tt // S5_BLOCK
    assert nb % 8 == 0
    nat = jnp.arange(tt)
    perm = (jnp.arange(tt)[None, :] == ((nat % HG_BLOCK) * (tt // HG_BLOCK) + nat // HG_BLOCK)[:, None]).astype(BF16)
    const2 = lambda b, t: (0, 0)
    const3 = lambda b, t: (0, 0, 0)
    layer3 = lambda b, t: (layer, 0, 0)
    layer4 = lambda b, t: (layer, 0, 0, 0)
    in_specs = [
        pl.BlockSpec(memory_space=pltpu.SMEM),
        pl.BlockSpec((1, tt, D_MODEL), lambda b, t: (b, t, 0)),
        pl.BlockSpec((1, D_MODEL), const2),
        pl.BlockSpec((None, D_MODEL, D_IN), layer3),
        pl.BlockSpec((None, D_MODEL, D_MODEL), layer3),
        pl.BlockSpec((CONV_W, GROUP_W), const2),
        pl.BlockSpec((None, S5_NG, 256, 384), layer4),
        pl.BlockSpec((None, S5_NG, 128, 256), layer4),
        pl.BlockSpec((3, S5_NG, 128), const3),
        pl.BlockSpec((1, GROUP_W), const2),
        pl.BlockSpec((None, GROUP_W, 2 * GROUP_W), layer3),
        pl.BlockSpec((1, 256), const2),
        pl.BlockSpec((1, 128), const2),
        pl.BlockSpec((1, GROUP_W), const2),
        pl.BlockSpec((1, GROUP_W), const2),
        pl.BlockSpec((1, D_MODEL), const2),
        pl.BlockSpec((tt, tt), const2),
    ]
    scratch = [
        pltpu.VMEM((N_CHUNKS, tt, LANES), F32),
        pltpu.VMEM((8 + tt, GROUP_W), F32),
        pltpu.VMEM((WIN_CHUNKS * CHUNK + tt, LANES), BF16),
        pltpu.VMEM((WIN_CHUNKS * CHUNK + tt, LANES), BF16),
        pltpu.VMEM((HG_DK, GROUP_W), F32),
        pltpu.VMEM((tt, GROUP_W), F32),
        pltpu.VMEM((2, S5_NG, LANES), F32),
        pltpu.VMEM((nb * S5_NG, LANES), F32),
        pltpu.VMEM((nb * S5_NG, LANES), F32),
        pltpu.VMEM((nb * S5_NG, LANES), F32),
        pltpu.VMEM((S5_NG, nb, GROUP_W), F32),
        pltpu.VMEM((2, tt, LANES), F32),
        pltpu.VMEM((tt, GROUP_W), F32),
        pltpu.VMEM((tt, D_MODEL), BF16),
    ]
    return pl.pallas_call(
        functools.partial(_mixer_kernel, tt=tt),
        grid=(bsz, seq // tt),
        in_specs=in_specs,
        out_specs=pl.BlockSpec((1, tt, D_MODEL), lambda b, t: (b, t, 0)),
        out_shape=jax.ShapeDtypeStruct(x.shape, x.dtype),
        scratch_shapes=scratch,
        compiler_params=pltpu.CompilerParams(
            dimension_semantics=("parallel", "arbitrary"),
            vmem_limit_bytes=VMEM_LIMIT_BYTES),
        name="mixer_layer",
    )(sinks, x, nm, w_in, w_out, conv_w, w1, cexp, pq, dskip, wglu, qg, kg, lb, hgn, gn, perm)


def _ffn_kernel(x_ref, nf_ref, w1_ref, w2_ref, o_ref, acc_sc):
    x = x_ref[...]
    ms = jnp.mean(x * x, axis=-1, keepdims=True)
    hn = (x * lax.rsqrt(ms + EPS) * nf_ref[...]).astype(BF16)
    for j in range(D_FF // FFN_COLS):
        a = jnp.maximum(_dot(hn, w1_ref[:, j * FFN_COLS:(j + 1) * FFN_COLS]), 0.0)
        part = _dot((a * a).astype(BF16), w2_ref[j * FFN_COLS:(j + 1) * FFN_COLS, :])
        if j == 0:
            acc_sc[...] = x + part
        else:
            acc_sc[...] += part
    o_ref[...] = acc_sc[...]


def _ffn_layer(layer, x2, nf, w1, w2):
    rows = x2.shape[0]
    tm = min(FFN_TILE, rows)
    assert rows % tm == 0
    return pl.pallas_call(
        _ffn_kernel,
        grid=(rows // tm,),
        in_specs=[
            pl.BlockSpec((tm, D_MODEL), lambda i: (i, 0)),
            pl.BlockSpec((1, D_MODEL), lambda i: (0, 0)),
            pl.BlockSpec((None, D_MODEL, D_FF), lambda i: (layer, 0, 0), pipeline_mode=pl.Buffered(1)),
            pl.BlockSpec((None, D_FF, D_MODEL), lambda i: (layer, 0, 0), pipeline_mode=pl.Buffered(1)),
        ],
        out_specs=pl.BlockSpec((tm, D_MODEL), lambda i: (i, 0)),
        out_shape=jax.ShapeDtypeStruct(x2.shape, x2.dtype),
        scratch_shapes=[pltpu.VMEM((tm, D_MODEL), F32)],
        compiler_params=pltpu.CompilerParams(
            dimension_semantics=("parallel",),
            vmem_limit_bytes=VMEM_LIMIT_BYTES),
        name="ffn_layer",
    )(x2, nf, w1, w2)


def _s5_tables(lam_re, lam_im, b_re, b_im, c_re, c_im, log_dt):
    hp = lax.Precision.HIGHEST
    lr = jnp.minimum(lam_re.astype(F32), -1e-4)
    li = lam_im.astype(F32)
    dt = jnp.exp(log_dt.astype(F32))[:, None]
    mag = jnp.exp(lr * dt)
    ar = mag * jnp.cos(li * dt)
    ai = mag * jnp.sin(li * dt)
    den = lr * lr + li * li
    zr = ((ar - 1.0) * lr + ai * li) / den
    zi = (ai * lr - (ar - 1.0) * li) / den
    bre = b_re.astype(F32)
    bim = b_im.astype(F32)
    bbr = zr[..., None] * bre - zi[..., None] * bim
    bbi = zr[..., None] * bim + zi[..., None] * bre
    d = jnp.arange(S5_BLOCK + 1, dtype=F32)[:, None, None]
    magd = jnp.exp(d * (lr * dt))
    pr = magd * jnp.cos(d * (li * dt))
    pi = magd * jnp.sin(d * (li * dt))
    cre = c_re.astype(F32)[None]
    cim = c_im.astype(F32)[None]
    car = cre * pr[:, :, None, :] - cim * pi[:, :, None, :]
    cai = cre * pi[:, :, None, :] + cim * pr[:, :, None, :]
    c_a = jnp.concatenate([car, -cai], axis=3)[:S5_BLOCK].transpose(1, 0, 2, 3)
    kern = jnp.einsum('gmp,gpj->gmj', c_a.reshape(S5_NG, 256, 2 * S5_STATE),
                      jnp.concatenate([bbr, bbi], axis=1), precision=hp)
    kcat = jnp.swapaxes(kern, 1, 2)
    lane = jnp.arange(256)
    toep = jnp.stack([jnp.where(lane >= S5_GROUP * k, jnp.roll(kcat, S5_GROUP * k, axis=2), 0.0)
                      for k in range(S5_BLOCK)], axis=1)
    toep = toep.reshape(S5_NG, 256, 256)
    prk = jnp.flip(pr[:S5_BLOCK], axis=0).transpose(1, 0, 2)[:, :, None, :]
    pik = jnp.flip(pi[:S5_BLOCK], axis=0).transpose(1, 0, 2)[:, :, None, :]
    bbr_t = bbr.transpose(0, 2, 1)[:, None]
    bbi_t = bbi.transpose(0, 2, 1)[:, None]
    s_in = jnp.concatenate([prk * bbr_t - pik * bbi_t, prk * bbi_t + pik * bbr_t], axis=3)
    s_in = s_in.reshape(S5_NG, 256, 128)
    w1 = jnp.concatenate([toep, s_in], axis=2).astype(BF16)
    s_out = jnp.concatenate([car[1:], -cai[1:]], axis=3).transpose(1, 0, 2, 3)
    cexp = jnp.swapaxes(s_out.reshape(S5_NG, 256, 128), 1, 2).astype(BF16)
    p_rot = jnp.concatenate([pr[S5_BLOCK], pr[S5_BLOCK]], axis=-1)
    q_rot = jnp.concatenate([-pi[S5_BLOCK], pi[S5_BLOCK]], axis=-1)
    pq = jnp.stack([p_rot, q_rot, -q_rot])
    return w1, cexp, pq


def kernel(x, w_in, w_out, norm_mix, norm_ffn, conv_w, s5_lam_re, s5_lam_im, s5_b_re, s5_b_im, s5_c_re, s5_c_im, s5_d, s5_log_dt, s5_w_glu, attn_q_norm, attn_k_norm, attn_sinks, hg_lower_bounds, hg_out_norm, group_norm, w_ff1, w_ff2):
    bsz, seq, _ = x.shape
    depth = w_in.shape[0]
    lb_p = jax.nn.softmax(hg_lower_bounds.astype(F32), axis=0)
    lb_table = jnp.cumsum(lb_p, axis=0) - lb_p[0]
    def swap_mid_heads(a, base, axis):
        lo, mid, hi = base + ATT_HD, base + 2 * ATT_HD, base + 3 * ATT_HD
        h1 = lax.slice_in_dim(a, lo, mid, axis=axis)
        h2 = lax.slice_in_dim(a, mid, hi, axis=axis)
        a = lax.dynamic_update_slice_in_dim(a, h2, lo, axis=axis)
        return lax.dynamic_update_slice_in_dim(a, h1, mid, axis=axis)

    w1, cexp, pq = jax.vmap(_s5_tables)(s5_lam_re, s5_lam_im, s5_b_re, s5_b_im, s5_c_re, s5_c_im,
                                         s5_log_dt)
    w_in_b = swap_mid_heads(w_in.astype(BF16), J_AQ * LANES, 2)
    w_out_b = swap_mid_heads(w_out.astype(BF16), 2 * GROUP_W, 1)
    gn_b = swap_mid_heads(group_norm, 2 * GROUP_W, 1)
    w_glu_b = s5_w_glu.astype(BF16)
    w_ff1_b = w_ff1.astype(BF16)
    w_ff2_b = w_ff2.astype(BF16)
    qg = jnp.tile(attn_q_norm, (1, ATT_QH))
    kg = jnp.tile(attn_k_norm, (1, ATT_KVH))
    hgn = jnp.tile(hg_out_norm, (1, HG_HEADS))
    row = lambda a, l: a[l].reshape(1, -1)
    for l in range(depth):
        x = _mixer_layer(
            l, x, attn_sinks[l].astype(F32), row(norm_mix, l), w_in_b, w_out_b,
            conv_w[l].astype(F32), w1, cexp, pq[l], row(s5_d, l), w_glu_b,
            row(qg, l), row(kg, l), row(lb_table, l), row(hgn, l), row(gn_b, l))
        x = _ffn_layer(l, x.reshape(bsz * seq, D_MODEL), row(norm_ffn, l),
                       w_ff1_b, w_ff2_b).reshape(bsz, seq, D_MODEL)
    return x
```

```python
import functools

import jax
import jax.numpy as jnp
from jax import lax
from jax.experimental import pallas as pl
from jax.experimental.pallas import tpu as pltpu

F32 = jnp.float32
BF16 = jnp.bfloat16

D_MODEL = 1024
GROUP_W = 256
N_GROUPS = 4
CHUNK = 64
CONV_W = 3
S5_GROUP = 16
S5_NG = 16
S5_STATE = 64
S5_BLOCK = 16
S5_FOLD = S5_BLOCK * S5_GROUP
S5_RI = 2 * S5_STATE
ATT_HD = 64
ATT_QH = 4
ATT_KVH = 2
ATT_REP = 2
ATT_SCALE = ATT_HD ** -0.5
WIN_CHUNKS = 2
HG_HEADS = 4
HG_DK = 64
HG_BLOCK = 16
D_FF = 4096
D_IN = 2560
EPS = 1e-6
LANES = 128
MXU_COLS = 256

J_CVH, J_CVB, J_CVC, J_S5U = 0, 2, 4, 6
J_AQ, J_AK, J_AV = 8, 10, 11
J_HQ, J_HF, J_HI, J_HG = 12, 14, 16, 18
N_CHUNKS = D_IN // LANES

SEQ_TILE = 512
FFN_TILE = 1024
FFN_COLS = 1024
VMEM_LIMIT_BYTES = 56 * 1024 * 1024


def _dot(a, b):
    return jnp.dot(a, b, preferred_element_type=F32)


def _dot_nt(a, b):
    return lax.dot_general(a, b, (((1,), (1,)), ((), ())), preferred_element_type=F32)


def _dot_tn(a, b):
    return lax.dot_general(a, b, (((0,), (0,)), ((), ())), preferred_element_type=F32)


def _segment_ones(n, seg):
    r = lax.broadcasted_iota(jnp.int32, (n, n), 0) // seg
    c = lax.broadcasted_iota(jnp.int32, (n, n), 1) // seg
    return (r == c).astype(BF16)


def _chunk_transpose16(arrs):
    rows = arrs[0].shape[0]
    chunk = lax.broadcasted_iota(jnp.int32, (rows, S5_FOLD), 1) // S5_GROUP
    for j in range(4):
        s = 1 << j
        low = ((chunk >> j) & 1) == 0
        new = list(arrs)
        for i in range(16):
            if i & s:
                continue
            a, b = arrs[i], arrs[i + s]
            new[i] = jnp.where(low, a, pltpu.roll(b, S5_GROUP * s, axis=1))
            new[i + s] = jnp.where(low, pltpu.roll(a, S5_FOLD - S5_GROUP * s, axis=1), b)
        arrs = new
    return arrs


def _halves(ref, base):
    return jnp.concatenate([ref[base], ref[base + 1]], axis=1)


def _mixer_kernel(sinks_ref, x_ref, nm_ref, win_ref, wout_ref, convw_ref, w1_ref, cexp_ref,
                  pq_ref, dskip_ref, wglu_ref, qg_ref, kg_ref, lb_ref, hgn_ref, gn_ref, perm_ref,
                  o_ref,
                  proj_sc, zp_sc, kext_sc, vext_sc, hgs_sc, hgi_sc,
                  s5s_sc, s5c_sc, s5w_sc, s5i_sc, s5y_sc, s5o_sc, att_sc, ycat_sc,
                  *, tt):
    t = pl.program_id(1)
    nb = tt // S5_BLOCK
    nhb = tt // HG_BLOCK
    ncp = tt // CHUNK
    band = (WIN_CHUNKS + 1) * CHUNK
    hist = WIN_CHUNKS * CHUNK

    @pl.when(t == 0)
    def _reset():
        zp_sc[0:8, :] = jnp.zeros((8, GROUP_W), F32)
        kext_sc[0:hist, :] = jnp.zeros((hist, LANES), BF16)
        vext_sc[0:hist, :] = jnp.zeros((hist, LANES), BF16)
        hgs_sc[...] = jnp.zeros((HG_DK, GROUP_W), F32)
        s5s_sc[...] = jnp.zeros((2, S5_NG, LANES), F32)

    x = x_ref[0]
    ms = jnp.mean(x * x, axis=-1, keepdims=True)
    hn = (x * lax.rsqrt(ms + EPS) * nm_ref[...]).astype(BF16)

    half_rows = tt // 2
    pieces = [(j * LANES, r0)
              for j in (J_S5U, J_HQ, J_HF, J_HI, J_HG, J_AQ, J_AK, J_CVH, J_CVB, J_CVC)
              for r0 in (0, half_rows)]
    n_s5, n_hg = 2, 10
    emitted = [0]

    def emit(n=1):
        for _ in range(n):
            if emitted[0] < len(pieces):
                c0, r0 = pieces[emitted[0]]
                emitted[0] += 1
                res = _dot(hn[r0:r0 + half_rows, :], win_ref[:, c0:c0 + MXU_COLS])
                for j in range(MXU_COLS // LANES):
                    proj_sc[c0 // LANES + j, r0:r0 + half_rows, :] = res[:, j * LANES:(j + 1) * LANES]

    def emit_until(k):
        emit(k - emitted[0])

    emit_until(n_s5)

    def group_norm(y, g):
        gain = gn_ref[:, g * GROUP_W:(g + 1) * GROUP_W]
        msq = jnp.mean(y * y, axis=-1, keepdims=True)
        return y * lax.rsqrt(msq + EPS) * gain

    ones256 = _segment_ones(GROUP_W, HG_DK)
    ones128 = _segment_ones(LANES, ATT_HD)

    folded = [jnp.concatenate([proj_sc[J_S5U + half, pl.ds(k, nb, stride=S5_BLOCK), :]
                               for half in range(2)], axis=1)
              for k in range(S5_BLOCK)]
    emit()
    per_group = _chunk_transpose16(folded)
    emit()
    for g in range(S5_NG):
        zg = _dot(per_group[g].astype(BF16), w1_ref[g])
        s5y_sc[g] = zg[:, 0:S5_FOLD]
        s5c_sc[pl.ds(g, nb, stride=S5_NG), :] = zg[:, S5_FOLD:S5_FOLD + S5_RI]
        if g % 8 == 7:
            emit()
    s5w_sc[...] = pltpu.roll(s5c_sc[...], 64, axis=1)
    p_rot, q_rot, q_swp = pq_ref[0], pq_ref[1], pq_ref[2]
    st, sw = s5s_sc[0], s5s_sc[1]
    for b in range(nb):
        s5i_sc[b * S5_NG:(b + 1) * S5_NG, :] = st
        c_b = s5c_sc[b * S5_NG:(b + 1) * S5_NG, :]
        w_b = s5w_sc[b * S5_NG:(b + 1) * S5_NG, :]
        st, sw = (p_rot * st + q_rot * sw + c_b, p_rot * sw + q_swp * st + w_b)
    s5s_sc[0] = st
    s5s_sc[1] = sw
    emit()
    outs = []
    for g in range(S5_NG):
        s_in = s5i_sc[pl.ds(g, nb, stride=S5_NG), :]
        outs.append(s5y_sc[g] + _dot(s_in.astype(BF16), cexp_ref[g]))
    emit()
    unfolded = _chunk_transpose16(outs)
    emit()
    for k in range(S5_BLOCK):
        for half in range(2):
            s5o_sc[half, pl.ds(k, nb, stride=S5_BLOCK), :] = (
                unfolded[k][:, LANES * half:LANES * (half + 1)])
    y5 = _halves(s5o_sc, 0) + dskip_ref[...] * _halves(proj_sc, J_S5U)
    y5 = jax.nn.gelu(y5)
    emit_until(n_hg)
    glu = _dot(y5.astype(BF16), wglu_ref[...])
    y_b = glu[:, 0:GROUP_W] * jax.nn.sigmoid(glu[:, GROUP_W:2 * GROUP_W])
    ycat_sc[:, GROUP_W:2 * GROUP_W] = group_norm(y_b, 1).astype(BF16)

    def pos_major(j):
        return jnp.concatenate(
            [jnp.concatenate([proj_sc[j + half, pl.ds(p, nhb, stride=HG_BLOCK), :]
                              for p in range(HG_BLOCK)], axis=0)
             for half in range(2)], axis=1)

    hq = pos_major(J_HQ)
    vv = pos_major(J_HI)
    lb = lb_ref[...]
    fg = lb + (1.0 - lb) * jax.nn.sigmoid(pos_major(J_HF))
    kk = 1.0 - fg
    lf = jnp.log(fg)
    cum = [lf[0:nhb, :]]
    for p in range(1, HG_BLOCK):
        cum.append(cum[-1] + lf[p * nhb:(p + 1) * nhb, :])
    bc = jnp.concatenate(cum, axis=0)
    b_last = cum[-1]
    gk = jnp.log(kk) - bc
    o_groups = []
    for p in range(HG_BLOCK):
        m = (p + 1) * nhb
        q_p = hq[p * nhb:m, :][None]
        b_p = bc[p * nhb:m, :][None]
        e = q_p * jnp.exp(b_p + gk[0:m, :].reshape(p + 1, nhb, GROUP_W))
        sc = _dot(e.reshape(m, GROUP_W).astype(BF16), ones256) * vv[0:m, :]
        o_groups.append(jnp.sum(sc.reshape(p + 1, nhb, GROUP_W), axis=0))
        if p >= 6:
            emit()
    o_pm = jnp.concatenate(o_groups, axis=0)
    emit_until(len(pieces))
    perm = perm_ref[...]
    qd = _dot(perm, (hq * jnp.exp(bc)).astype(BF16)).astype(BF16)
    k_end = kk * jnp.exp(jnp.concatenate([b_last] * HG_BLOCK, axis=0) - bc)
    k_end = _dot(perm, k_end.astype(BF16)).astype(BF16)
    o_hi = o_pm.astype(BF16)
    o_lo = (o_pm - o_hi.astype(F32)).astype(BF16)
    o_intra = _dot(perm, o_hi) + _dot(perm, o_lo)
    vv_n = _halves(proj_sc, J_HI).astype(BF16)
    lane_head = lax.broadcasted_iota(jnp.int32, (HG_BLOCK, GROUP_W), 1) // HG_DK
    head_masks = [lane_head == h for h in range(HG_HEADS)]
    dec = jnp.exp(b_last)

    def per_head_rows(a):
        return jnp.concatenate([jnp.where(m, a, jnp.zeros_like(a)) for m in head_masks], axis=0)

    upds = []
    for n in range(nhb):
        rows = slice(n * HG_BLOCK, (n + 1) * HG_BLOCK)
        v_n = vv_n[rows, :]
        v_stk = jnp.concatenate([v_n[:, h * HG_DK:(h + 1) * HG_DK] for h in range(HG_HEADS)], axis=0)
        upds.append(_dot_tn(v_stk, per_head_rows(k_end[rows, :])))
    s_val = hgs_sc[...]
    states = []
    for n in range(nhb):
        states.append(s_val.astype(BF16))
        s_val = s_val * dec[n:n + 1, :] + upds[n]
    hgs_sc[...] = s_val
    for n in range(nhb):
        rows = slice(n * HG_BLOCK, (n + 1) * HG_BLOCK)
        out = _dot_nt(per_head_rows(qd[rows, :]), states[n])
        hgi_sc[rows, :] = jnp.concatenate(
            [out[h * HG_BLOCK:(h + 1) * HG_BLOCK, :] for h in range(HG_HEADS)], axis=1)
    o_h = o_intra + hgi_sc[...]
    o_ms = _dot((o_h * o_h).astype(BF16), ones256) * (1.0 / HG_DK)
    gate = _halves(proj_sc, J_HG)
    y_d = o_h * lax.rsqrt(o_ms + EPS) * hgn_ref[...] * (gate * jax.nn.sigmoid(gate))
    ycat_sc[:, 3 * GROUP_W:4 * GROUP_W] = group_norm(y_d, 3).astype(BF16)

    z = _halves(proj_sc, J_CVC) * _halves(proj_sc, J_CVH)
    zp_sc[8:8 + tt, :] = z
    acc = (convw_ref[0:1, :] * zp_sc[6:6 + tt, :] + convw_ref[1:2, :] * zp_sc[7:7 + tt, :]
           + convw_ref[2:3, :] * z)
    y_a = _halves(proj_sc, J_CVB) * acc
    zp_sc[0:8, :] = zp_sc[tt:tt + 8, :]
    ycat_sc[:, 0:GROUP_W] = group_norm(y_a, 0).astype(BF16)

    aq = _halves(proj_sc, J_AQ)
    ak = proj_sc[J_AK]
    q_ms = _dot((aq * aq).astype(BF16), ones256) * (1.0 / ATT_HD)
    k_ms = _dot((ak * ak).astype(BF16), ones128) * (1.0 / ATT_HD)
    qn = aq * lax.rsqrt(q_ms + EPS) * qg_ref[...] * ATT_SCALE
    kn = ak * lax.rsqrt(k_ms + EPS) * kg_ref[...]
    kext_sc[hist:hist + tt, :] = kn.astype(BF16)
    vext_sc[hist:hist + tt, :] = proj_sc[J_AV].astype(BF16)
    lane_kv = lax.broadcasted_iota(jnp.int32, (tt, LANES), 1) // ATT_HD
    q_masked = [[jnp.where(lane_kv == g, qn[:, r * LANES:(r + 1) * LANES], 0.0).astype(BF16)
                 for r in range(ATT_REP)] for g in range(ATT_KVH)]
    col = lax.broadcasted_iota(jnp.int32, (ATT_REP * CHUNK, band), 1)
    row = lax.broadcasted_iota(jnp.int32, (ATT_REP * CHUNK, 1), 0)
    first_kv = lax.broadcasted_iota(jnp.int32, (CHUNK, LANES), 1) < ATT_HD
    blocks = [(c, g) for c in range(ncp) for g in range(ATT_KVH)]
    scores = []
    for c, g in blocks:
        q2 = jnp.concatenate([q_masked[g][r][c * CHUNK:(c + 1) * CHUNK, :] for r in range(ATT_REP)],
                             axis=0)
        scores.append(_dot_nt(q2, kext_sc[c * CHUNK:c * CHUNK + band, :]))
    probs, dens = [], []
    for (c, g), s in zip(blocks, scores):
        if c < WIN_CHUNKS:
            first_valid = jnp.maximum(WIN_CHUNKS - (t * ncp + c), 0) * CHUNK
            s = jnp.where(col >= first_valid, s, -jnp.inf)
        sink = jnp.where(row < CHUNK, sinks_ref[ATT_REP * g], sinks_ref[ATT_REP * g + 1])
        m = jnp.maximum(jnp.max(s, axis=-1, keepdims=True), sink)
        p = jnp.exp(s - m)
        dens.append(jnp.sum(p, axis=-1, keepdims=True) + jnp.exp(sink - m))
        probs.append(p.astype(BF16))
    outs_kv = {}
    for (c, g), p, den in zip(blocks, probs, dens):
        outs_kv[c, g] = _dot(p, vext_sc[c * CHUNK:c * CHUNK + band, :]) / den
    for c in range(ncp):
        for r in range(ATT_REP):
            att_sc[c * CHUNK:(c + 1) * CHUNK, r * LANES:(r + 1) * LANES] = jnp.where(
                first_kv, outs_kv[c, 0][r * CHUNK:(r + 1) * CHUNK, :],
                outs_kv[c, 1][r * CHUNK:(r + 1) * CHUNK, :])
    kext_sc[0:hist, :] = kext_sc[tt:tt + hist, :]
    vext_sc[0:hist, :] = vext_sc[tt:tt + hist, :]
    ycat_sc[:, 2 * GROUP_W:3 * GROUP_W] = group_norm(att_sc[...], 2).astype(BF16)

    o_ref[0] = x + _dot(ycat_sc[...], wout_ref[...])


def _mixer_layer(layer, x, sinks, nm, w_in, w_out, conv_w, w1, cexp, pq, dskip, wglu, qg, kg, lb, hgn, gn):
    bsz, seq, _ = x.shape
    tt = min(SEQ_TILE, seq)
    assert seq % tt == 0 and tt % CHUNK == 0 and tt >= WIN_CHUNKS * CHUNK
    nb = tt // S5_BLOCK
    assert nb % 8 == 0
    nat = jnp.arange(tt)
    perm = (jnp.arange(tt)[None, :] == ((nat % HG_BLOCK) * (tt // HG_BLOCK) + nat // HG_BLOCK)[:, None]).astype(BF16)
    const2 = lambda b, t: (0, 0)
    const3 = lambda b, t: (0, 0, 0)
    layer3 = lambda b, t: (layer, 0, 0)
    layer4 = lambda b, t: (layer, 0, 0, 0)
    in_specs = [
        pl.BlockSpec(memory_space=pltpu.SMEM),
        pl.BlockSpec((1, tt, D_MODEL), lambda b, t: (b, t, 0)),
        pl.BlockSpec((1, D_MODEL), const2),
        pl.BlockSpec((None, D_MODEL, D_IN), layer3),
        pl.BlockSpec((None, D_MODEL, D_MODEL), layer3),
        pl.BlockSpec((CONV_W, GROUP_W), const2),
        pl.BlockSpec((None, S5_NG, S5_FOLD, S5_FOLD + S5_RI), layer4),
        pl.BlockSpec((None, S5_NG, S5_RI, S5_FOLD), layer4),
        pl.BlockSpec((3, S5_NG, S5_RI), const3),
        pl.BlockSpec((1, GROUP_W), const2),
        pl.BlockSpec((None, GROUP_W, 2 * GROUP_W), layer3),
        pl.BlockSpec((1, ATT_QH * ATT_HD), const2),
        pl.BlockSpec((1, ATT_KVH * ATT_HD), const2),
        pl.BlockSpec((1, GROUP_W), const2),
        pl.BlockSpec((1, GROUP_W), const2),
        pl.BlockSpec((1, D_MODEL), const2),
        pl.BlockSpec((tt, tt), const2),
    ]
    scratch = [
        pltpu.VMEM((N_CHUNKS, tt, LANES), F32),
        pltpu.VMEM((8 + tt, GROUP_W), F32),
        pltpu.VMEM((WIN_CHUNKS * CHUNK + tt, LANES), BF16),
        pltpu.VMEM((WIN_CHUNKS * CHUNK + tt, LANES), BF16),
        pltpu.VMEM((HG_DK, GROUP_W), F32),
        pltpu.VMEM((tt, GROUP_W), F32),
        pltpu.VMEM((2, S5_NG, LANES), F32),
        pltpu.VMEM((nb * S5_NG, LANES), F32),
        pltpu.VMEM((nb * S5_NG, LANES), F32),
        pltpu.VMEM((nb * S5_NG, LANES), F32),
        pltpu.VMEM((S5_NG, nb, GROUP_W), F32),
        pltpu.VMEM((2, tt, LANES), F32),
        pltpu.VMEM((tt, GROUP_W), F32),
        pltpu.VMEM((tt, D_MODEL), BF16),
    ]
    return pl.pallas_call(
        functools.partial(_mixer_kernel, tt=tt),
        grid=(bsz, seq // tt),
        in_specs=in_specs,
        out_specs=pl.BlockSpec((1, tt, D_MODEL), lambda b, t: (b, t, 0)),
        out_shape=jax.ShapeDtypeStruct(x.shape, x.dtype),
        scratch_shapes=scratch,
        compiler_params=pltpu.CompilerParams(
            dimension_semantics=("parallel", "arbitrary"),
            vmem_limit_bytes=VMEM_LIMIT_BYTES),
        name="mixer_layer",
    )(sinks, x, nm, w_in, w_out, conv_w, w1, cexp, pq, dskip, wglu, qg, kg, lb, hgn, gn, perm)


def _ffn_kernel(x_ref, nf_ref, w1_ref, w2_ref, o_ref, acc_sc):
    x = x_ref[...]
    ms = jnp.mean(x * x, axis=-1, keepdims=True)
    hn = (x * lax.rsqrt(ms + EPS) * nf_ref[...]).astype(BF16)
    for j in range(D_FF // FFN_COLS):
        a = jnp.maximum(_dot(hn, w1_ref[:, j * FFN_COLS:(j + 1) * FFN_COLS]), 0.0)
        part = _dot((a * a).astype(BF16), w2_ref[j * FFN_COLS:(j + 1) * FFN_COLS, :])
        if j == 0:
            acc_sc[...] = x + part
        else:
            acc_sc[...] += part
    o_ref[...] = acc_sc[...]


def _ffn_layer(layer, x2, nf, w1, w2):
    rows = x2.shape[0]
    tm = min(FFN_TILE, rows)
    assert rows % tm == 0
    return pl.pallas_call(
        _ffn_kernel,
        grid=(rows // tm,),
        in_specs=[
            pl.BlockSpec((tm, D_MODEL), lambda i: (i, 0)),
            pl.BlockSpec((1, D_MODEL), lambda i: (0, 0)),
            pl.BlockSpec((None, D_MODEL, D_FF), lambda i: (layer, 0, 0), pipeline_mode=pl.Buffered(1)),
            pl.BlockSpec((None, D_FF, D_MODEL), lambda i: (layer, 0, 0), pipeline_mode=pl.Buffered(1)),
        ],
        out_specs=pl.BlockSpec((tm, D_MODEL), lambda i: (i, 0)),
        out_shape=jax.ShapeDtypeStruct(x2.shape, x2.dtype),
        scratch_shapes=[pltpu.VMEM((tm, D_MODEL), F32)],
        compiler_params=pltpu.CompilerParams(
            dimension_semantics=("parallel",),
            vmem_limit_bytes=VMEM_LIMIT_BYTES),
        name="ffn_layer",
    )(x2, nf, w1, w2)


def _s5_tables(lam_re, lam_im, b_re, b_im, c_re, c_im, log_dt):
    hp = lax.Precision.HIGHEST
    lr = jnp.minimum(lam_re.astype(F32), -1e-4)
    li = lam_im.astype(F32)
    dt = jnp.exp(log_dt.astype(F32))[:, None]
    mag = jnp.exp(lr * dt)
    ar = mag * jnp.cos(li * dt)
    ai = mag * jnp.sin(li * dt)
    den = lr * lr + li * li
    zr = ((ar - 1.0) * lr + ai * li) / den
    zi = (ai * lr - (ar - 1.0) * li) / den
    bre = b_re.astype(F32)
    bim = b_im.astype(F32)
    bbr = zr[..., None] * bre - zi[..., None] * bim
    bbi = zr[..., None] * bim + zi[..., None] * bre
    d = jnp.arange(S5_BLOCK + 1, dtype=F32)[:, None, None]
    magd = jnp.exp(d * (lr * dt))
    pr = magd * jnp.cos(d * (li * dt))
    pi = magd * jnp.sin(d * (li * dt))
    cre = c_re.astype(F32)[None]
    cim = c_im.astype(F32)[None]
    car = cre * pr[:, :, None, :] - cim * pi[:, :, None, :]
    cai = cre * pi[:, :, None, :] + cim * pr[:, :, None, :]
    c_a = jnp.concatenate([car, -cai], axis=3)[:S5_BLOCK].transpose(1, 0, 2, 3)
    kern = jnp.einsum('gmp,gpj->gmj', c_a.reshape(S5_NG, S5_FOLD, S5_RI),
                      jnp.concatenate([bbr, bbi], axis=1), precision=hp)
    kcat = jnp.swapaxes(kern, 1, 2)
    lane = jnp.arange(S5_FOLD)
    toep = jnp.stack([jnp.where(lane >= S5_GROUP * k, jnp.roll(kcat, S5_GROUP * k, axis=2), 0.0)
                      for k in range(S5_BLOCK)], axis=1)
    toep = toep.reshape(S5_NG, S5_FOLD, S5_FOLD)
    prk = jnp.flip(pr[:S5_BLOCK], axis=0).transpose(1, 0, 2)[:, :, None, :]
    pik = jnp.flip(pi[:S5_BLOCK], axis=0).transpose(1, 0, 2)[:, :, None, :]
    bbr_t = bbr.transpose(0, 2, 1)[:, None]
    bbi_t = bbi.transpose(0, 2, 1)[:, None]
    s_in = jnp.concatenate([prk * bbr_t - pik * bbi_t, prk * bbi_t + pik * bbr_t], axis=3)
    s_in = s_in.reshape(S5_NG, S5_FOLD, S5_RI)
    w1 = jnp.concatenate([toep, s_in], axis=2).astype(BF16)
    s_out = jnp.concatenate([car[1:], -cai[1:]], axis=3).transpose(1, 0, 2, 3)
    cexp = jnp.swapaxes(s_out.reshape(S5_NG, S5_FOLD, S5_RI), 1, 2).astype(BF16)
    p_rot = jnp.concatenate([pr[S5_BLOCK], pr[S5_BLOCK]], axis=-1)
    q_rot = jnp.concatenate([-pi[S5_BLOCK], pi[S5_BLOCK]], axis=-1)
    pq = jnp.stack([p_rot, q_rot, -q_rot])
    return w1, cexp, pq


def kernel(x, w_in, w_out, norm_mix, norm_ffn, conv_w, s5_lam_re, s5_lam_im, s5_b_re, s5_b_im, s5_c_re, s5_c_im, s5_d, s5_log_dt, s5_w_glu, attn_q_norm, attn_k_norm, attn_sinks, hg_lower_bounds, hg_out_norm, group_norm, w_ff1, w_ff2):
    bsz, seq, _ = x.shape
    depth = w_in.shape[0]
    lb_p = jax.nn.softmax(hg_lower_bounds.astype(F32), axis=0)
    lb_table = jnp.cumsum(lb_p, axis=0) - lb_p[0]
    def swap_mid_heads(a, base, axis):
        lo, mid, hi = base + ATT_HD, base + 2 * ATT_HD, base + 3 * ATT_HD
        h1 = lax.slice_in_dim(a, lo, mid, axis=axis)
        h2 = lax.slice_in_dim(a, mid, hi, axis=axis)
        a = lax.dynamic_update_slice_in_dim(a, h2, lo, axis=axis)
        return lax.dynamic_update_slice_in_dim(a, h1, mid, axis=axis)

    w1, cexp, pq = jax.vmap(_s5_tables)(s5_lam_re, s5_lam_im, s5_b_re, s5_b_im, s5_c_re, s5_c_im,
                                         s5_log_dt)
    w_in_b = swap_mid_heads(w_in.astype(BF16), J_AQ * LANES, 2)
    w_out_b = swap_mid_heads(w_out.astype(BF16), 2 * GROUP_W, 1)
    gn_b = swap_mid_heads(group_norm, 2 * GROUP_W, 1)
    w_glu_b = s5_w_glu.astype(BF16)
    w_ff1_b = w_ff1.astype(BF16)
    w_ff2_b = w_ff2.astype(BF16)
    qg = jnp.tile(attn_q_norm, (1, ATT_QH))
    kg = jnp.tile(attn_k_norm, (1, ATT_KVH))
    hgn = jnp.tile(hg_out_norm, (1, HG_HEADS))
    row = lambda a, l: a[l].reshape(1, -1)
    for l in range(depth):
        x = _mixer_layer(
            l, x, attn_sinks[l].astype(F32), row(norm_mix, l), w_in_b, w_out_b,
            conv_w[l].astype(F32), w1, cexp, pq[l], row(s5_d, l), w_glu_b,
            row(qg, l), row(kg, l), row(lb_table, l), row(hgn, l), row(gn_b, l))
        x = _ffn_layer(l, x.reshape(bsz * seq, D_MODEL), row(norm_ffn, l),
                       w_ff1_b, w_ff2_b).reshape(bsz, seq, D_MODEL)
    return x
```

```python
import functools

import jax
import jax.numpy as jnp
from jax import lax
from jax.experimental import pallas as pl
from jax.experimental.pallas import tpu as pltpu

F32 = jnp.float32
BF16 = jnp.bfloat16

D_MODEL = 1024
GROUP_W = 256
N_GROUPS = 4
CHUNK = 64
CONV_W = 3
S5_GROUP = 16
S5_NG = 16
S5_STATE = 64
S5_BLOCK = 16
S5_FOLD = S5_BLOCK * S5_GROUP
S5_RI = 2 * S5_STATE
ATT_HD = 64
ATT_QH = 4
ATT_KVH = 2
ATT_REP = 2
ATT_SCALE = ATT_HD ** -0.5
WIN_CHUNKS = 2
HG_HEADS = 4
HG_DK = 64
HG_BLOCK = 16
D_FF = 4096
D_IN = 2560
EPS = 1e-6
LANES = 128
MXU_COLS = 256

J_CVH, J_CVB, J_CVC, J_S5U = 0, 2, 4, 6
J_AQ, J_AK, J_AV = 8, 10, 11
J_HQ, J_HF, J_HI, J_HG = 12, 14, 16, 18
N_CHUNKS = D_IN // LANES

SEQ_TILE = 512
FFN_TILE = 1024
FFN_COLS = 1024
VMEM_LIMIT_BYTES = 56 * 1024 * 1024


def _dot(a, b):
    return jnp.dot(a, b, preferred_element_type=F32)


def _dot_nt(a, b):
    return lax.dot_general(a, b, (((1,), (1,)), ((), ())), preferred_element_type=F32)


def _dot_tn(a, b):
    return lax.dot_general(a, b, (((0,), (0,)), ((), ())), preferred_element_type=F32)


def _segment_ones(n, seg):
    r = lax.broadcasted_iota(jnp.int32, (n, n), 0) // seg
    c = lax.broadcasted_iota(jnp.int32, (n, n), 1) // seg
    return (r == c).astype(BF16)


def _chunk_transpose16(arrs):
    rows = arrs[0].shape[0]
    chunk = lax.broadcasted_iota(jnp.int32, (rows, S5_FOLD), 1) // S5_GROUP
    for j in range(4):
        s = 1 << j
        low = ((chunk >> j) & 1) == 0
        new = list(arrs)
        for i in range(16):
            if i & s:
                continue
            a, b = arrs[i], arrs[i + s]
            new[i] = jnp.where(low, a, pltpu.roll(b, S5_GROUP * s, axis=1))
            new[i + s] = jnp.where(low, pltpu.roll(a, S5_FOLD - S5_GROUP * s, axis=1), b)
        arrs = new
    return arrs


def _halves(ref, base):
    return jnp.concatenate([ref[base], ref[base + 1]], axis=1)


def _mixer_kernel(sinks_ref, x_ref, nm_ref, win_ref, wout_ref, convw_ref, w1_ref, cexp_ref,
                  pq_ref, dskip_ref, wglu_ref, qg_ref, kg_ref, lb_ref, hgn_ref, gn_ref,
                  o_ref,
                  proj_sc, zp_sc, kext_sc, vext_sc, hgs_sc, hgi_sc, hgo_sc, hgq_sc, hgk_sc,
                  s5s_sc, s5c_sc, s5w_sc, s5i_sc, s5y_sc, s5o_sc, att_sc, ycat_sc,
                  *, tt):
    t = pl.program_id(1)
    nb = tt // S5_BLOCK
    nhb = tt // HG_BLOCK
    ncp = tt // CHUNK
    band = (WIN_CHUNKS + 1) * CHUNK
    hist = WIN_CHUNKS * CHUNK

    @pl.when(t == 0)
    def _reset():
        zp_sc[0:8, :] = jnp.zeros((8, GROUP_W), F32)
        kext_sc[0:hist, :] = jnp.zeros((hist, LANES), BF16)
        vext_sc[0:hist, :] = jnp.zeros((hist, LANES), BF16)
        hgs_sc[...] = jnp.zeros((HG_DK, GROUP_W), F32)
        s5s_sc[...] = jnp.zeros((2, S5_NG, LANES), F32)

    x = x_ref[0]
    ms = jnp.mean(x * x, axis=-1, keepdims=True)
    hn = (x * lax.rsqrt(ms + EPS) * nm_ref[...]).astype(BF16)

    half_rows = tt // 2
    pieces = [(j * LANES, r0)
              for j in (J_S5U, J_HQ, J_HF, J_HI, J_HG, J_AQ, J_AK, J_CVH, J_CVB, J_CVC)
              for r0 in (0, half_rows)]
    n_s5, n_hg = 2, 10
    emitted = [0]

    def emit(n=1):
        for _ in range(n):
            if emitted[0] < len(pieces):
                c0, r0 = pieces[emitted[0]]
                emitted[0] += 1
                res = _dot(hn[r0:r0 + half_rows, :], win_ref[:, c0:c0 + MXU_COLS])
                for j in range(MXU_COLS // LANES):
                    proj_sc[c0 // LANES + j, r0:r0 + half_rows, :] = res[:, j * LANES:(j + 1) * LANES]

    def emit_until(k):
        emit(k - emitted[0])

    emit_until(n_s5)

    def group_norm(y, g):
        gain = gn_ref[:, g * GROUP_W:(g + 1) * GROUP_W]
        msq = jnp.mean(y * y, axis=-1, keepdims=True)
        return y * lax.rsqrt(msq + EPS) * gain

    ones256 = _segment_ones(GROUP_W, HG_DK)
    ones128 = _segment_ones(LANES, ATT_HD)

    folded = [jnp.concatenate([proj_sc[J_S5U + half, pl.ds(k, nb, stride=S5_BLOCK), :]
                               for half in range(2)], axis=1)
              for k in range(S5_BLOCK)]
    emit()
    per_group = _chunk_transpose16(folded)
    emit()
    for g in range(S5_NG):
        zg = _dot(per_group[g].astype(BF16), w1_ref[g])
        s5y_sc[g] = zg[:, 0:S5_FOLD]
        s5c_sc[pl.ds(g, nb, stride=S5_NG), :] = zg[:, S5_FOLD:S5_FOLD + S5_RI]
        if g % 8 == 7:
            emit()
    s5w_sc[...] = pltpu.roll(s5c_sc[...], 64, axis=1)
    p_rot, q_rot, q_swp = pq_ref[0], pq_ref[1], pq_ref[2]
    st, sw = s5s_sc[0], s5s_sc[1]
    for b in range(nb):
        s5i_sc[b * S5_NG:(b + 1) * S5_NG, :] = st
        c_b = s5c_sc[b * S5_NG:(b + 1) * S5_NG, :]
        w_b = s5w_sc[b * S5_NG:(b + 1) * S5_NG, :]
        st, sw = (p_rot * st + q_rot * sw + c_b, p_rot * sw + q_swp * st + w_b)
    s5s_sc[0] = st
    s5s_sc[1] = sw
    emit()
    outs = []
    for g in range(S5_NG):
        s_in = s5i_sc[pl.ds(g, nb, stride=S5_NG), :]
        outs.append(s5y_sc[g] + _dot(s_in.astype(BF16), cexp_ref[g]))
    emit()
    unfolded = _chunk_transpose16(outs)
    emit()
    for k in range(S5_BLOCK):
        for half in range(2):
            s5o_sc[half, pl.ds(k, nb, stride=S5_BLOCK), :] = (
                unfolded[k][:, LANES * half:LANES * (half + 1)])
    y5 = _halves(s5o_sc, 0) + dskip_ref[...] * _halves(proj_sc, J_S5U)
    y5 = jax.nn.gelu(y5)
    emit_until(n_hg)
    glu = _dot(y5.astype(BF16), wglu_ref[...])
    y_b = glu[:, 0:GROUP_W] * jax.nn.sigmoid(glu[:, GROUP_W:2 * GROUP_W])
    ycat_sc[:, GROUP_W:2 * GROUP_W] = group_norm(y_b, 1).astype(BF16)

    def pos_major(j):
        return jnp.concatenate(
            [jnp.concatenate([proj_sc[j + half, pl.ds(p, nhb, stride=HG_BLOCK), :]
                              for p in range(HG_BLOCK)], axis=0)
             for half in range(2)], axis=1)

    hq = pos_major(J_HQ)
    vv = pos_major(J_HI)
    lb = lb_ref[...]
    fg = lb + (1.0 - lb) * jax.nn.sigmoid(pos_major(J_HF))
    kk = 1.0 - fg
    lf = jnp.log(fg)
    cum = [lf[0:nhb, :]]
    for p in range(1, HG_BLOCK):
        cum.append(cum[-1] + lf[p * nhb:(p + 1) * nhb, :])
    bc = jnp.concatenate(cum, axis=0)
    b_last = cum[-1]
    gk = jnp.log(kk) - bc

    def to_natural(dst_sc, p, group):
        for half in range(2):
            dst_sc[half, pl.ds(p, nhb, stride=HG_BLOCK), :] = group[:, half * LANES:(half + 1) * LANES]

    for p in range(HG_BLOCK):
        m = (p + 1) * nhb
        q_p = hq[p * nhb:m, :][None]
        b_p = bc[p * nhb:m, :][None]
        e = q_p * jnp.exp(b_p + gk[0:m, :].reshape(p + 1, nhb, GROUP_W))
        sc = _dot(e.reshape(m, GROUP_W).astype(BF16), ones256) * vv[0:m, :]
        to_natural(hgo_sc, p, jnp.sum(sc.reshape(p + 1, nhb, GROUP_W), axis=0))
        if p >= 6:
            emit()
    emit_until(len(pieces))
    for p in range(HG_BLOCK):
        to_natural(hgq_sc, p, hq[p * nhb:(p + 1) * nhb, :] * jnp.exp(cum[p]))
        to_natural(hgk_sc, p, kk[p * nhb:(p + 1) * nhb, :] * jnp.exp(b_last - cum[p]))
    qd = _halves(hgq_sc, 0).astype(BF16)
    k_end = _halves(hgk_sc, 0).astype(BF16)
    o_intra = _halves(hgo_sc, 0)
    vv_n = _halves(proj_sc, J_HI).astype(BF16)
    lane_head = lax.broadcasted_iota(jnp.int32, (HG_BLOCK, GROUP_W), 1) // HG_DK
    head_masks = [lane_head == h for h in range(HG_HEADS)]
    dec = jnp.exp(b_last)

    def per_head_rows(a):
        return jnp.concatenate([jnp.where(m, a, jnp.zeros_like(a)) for m in head_masks], axis=0)

    upds = []
    for n in range(nhb):
        rows = slice(n * HG_BLOCK, (n + 1) * HG_BLOCK)
        v_n = vv_n[rows, :]
        v_stk = jnp.concatenate([v_n[:, h * HG_DK:(h + 1) * HG_DK] for h in range(HG_HEADS)], axis=0)
        upds.append(_dot_tn(v_stk, per_head_rows(k_end[rows, :])))
    s_val = hgs_sc[...]
    states = []
    for n in range(nhb):
        states.append(s_val.astype(BF16))
        s_val = s_val * dec[n:n + 1, :] + upds[n]
    hgs_sc[...] = s_val
    for n in range(nhb):
        rows = slice(n * HG_BLOCK, (n + 1) * HG_BLOCK)
        out = _dot_nt(per_head_rows(qd[rows, :]), states[n])
        hgi_sc[rows, :] = jnp.concatenate(
            [out[h * HG_BLOCK:(h + 1) * HG_BLOCK, :] for h in range(HG_HEADS)], axis=1)
    o_h = o_intra + hgi_sc[...]
    o_ms = _dot((o_h * o_h).astype(BF16), ones256) * (1.0 / HG_DK)
    gate = _halves(proj_sc, J_HG)
    y_d = o_h * lax.rsqrt(o_ms + EPS) * hgn_ref[...] * (gate * jax.nn.sigmoid(gate))
    ycat_sc[:, 3 * GROUP_W:4 * GROUP_W] = group_norm(y_d, 3).astype(BF16)

    z = _halves(proj_sc, J_CVC) * _halves(proj_sc, J_CVH)
    zp_sc[8:8 + tt, :] = z
    acc = (convw_ref[0:1, :] * zp_sc[6:6 + tt, :] + convw_ref[1:2, :] * zp_sc[7:7 + tt, :]
           + convw_ref[2:3, :] * z)
    y_a = _halves(proj_sc, J_CVB) * acc
    zp_sc[0:8, :] = zp_sc[tt:tt + 8, :]
    ycat_sc[:, 0:GROUP_W] = group_norm(y_a, 0).astype(BF16)

    aq = _halves(proj_sc, J_AQ)
    ak = proj_sc[J_AK]
    q_ms = _dot((aq * aq).astype(BF16), ones256) * (1.0 / ATT_HD)
    k_ms = _dot((ak * ak).astype(BF16), ones128) * (1.0 / ATT_HD)
    qn = aq * lax.rsqrt(q_ms + EPS) * qg_ref[...] * ATT_SCALE
    kn = ak * lax.rsqrt(k_ms + EPS) * kg_ref[...]
    kext_sc[hist:hist + tt, :] = kn.astype(BF16)
    vext_sc[hist:hist + tt, :] = proj_sc[J_AV].astype(BF16)
    lane_kv = lax.broadcasted_iota(jnp.int32, (tt, LANES), 1) // ATT_HD
    q_masked = [[jnp.where(lane_kv == g, qn[:, r * LANES:(r + 1) * LANES], 0.0).astype(BF16)
                 for r in range(ATT_REP)] for g in range(ATT_KVH)]
    col = lax.broadcasted_iota(jnp.int32, (ATT_REP * CHUNK, band), 1)
    row = lax.broadcasted_iota(jnp.int32, (ATT_REP * CHUNK, 1), 0)
    first_kv = lax.broadcasted_iota(jnp.int32, (CHUNK, LANES), 1) < ATT_HD
    blocks = [(c, g) for c in range(ncp) for g in range(ATT_KVH)]
    scores = []
    for c, g in blocks:
        q2 = jnp.concatenate([q_masked[g][r][c * CHUNK:(c + 1) * CHUNK, :] for r in range(ATT_REP)],
                             axis=0)
        scores.append(_dot_nt(q2, kext_sc[c * CHUNK:c * CHUNK + band, :]))
    probs, dens = [], []
    for (c, g), s in zip(blocks, scores):
        if c < WIN_CHUNKS:
            first_valid = jnp.maximum(WIN_CHUNKS - (t * ncp + c), 0) * CHUNK
            s = jnp.where(col >= first_valid, s, -jnp.inf)
        sink = jnp.where(row < CHUNK, sinks_ref[ATT_REP * g], sinks_ref[ATT_REP * g + 1])
        m = jnp.maximum(jnp.max(s, axis=-1, keepdims=True), sink)
        p = jnp.exp(s - m)
        dens.append(jnp.sum(p, axis=-1, keepdims=True) + jnp.exp(sink - m))
        probs.append(p.astype(BF16))
    outs_kv = {}
    for (c, g), p, den in zip(blocks, probs, dens):
        outs_kv[c, g] = _dot(p, vext_sc[c * CHUNK:c * CHUNK + band, :]) / den
    for c in range(ncp):
        for r in range(ATT_REP):
            att_sc[c * CHUNK:(c + 1) * CHUNK, r * LANES:(r + 1) * LANES] = jnp.where(
                first_kv, outs_kv[c, 0][r * CHUNK:(r + 1) * CHUNK, :],
                outs_kv[c, 1][r * CHUNK:(r + 1) * CHUNK, :])
    kext_sc[0:hist, :] = kext_sc[tt:tt + hist, :]
    vext_sc[0:hist, :] = vext_sc[tt:tt + hist, :]
    ycat_sc[:, 2 * GROUP_W:3 * GROUP_W] = group_norm(att_sc[...], 2).astype(BF16)

    o_ref[0] = x + _dot(ycat_sc[...], wout_ref[...])


def _mixer_layer(layer, x, sinks, nm, w_in, w_out, conv_w, w1, cexp, pq, dskip, wglu, qg, kg, lb, hgn, gn):
    bsz, seq, _ = x.shape
    tt = min(SEQ_TILE, seq)
    assert seq % tt == 0 and tt % CHUNK == 0 and tt >= WIN_CHUNKS * CHUNK
    nb = tt // S5_BLOCK
    assert nb % 8 == 0
    const2 = lambda b, t: (0, 0)
    const3 = lambda b, t: (0, 0, 0)
    layer3 = lambda b, t: (layer, 0, 0)
    layer4 = lambda b, t: (layer, 0, 0, 0)
    in_specs = [
        pl.BlockSpec(memory_space=pltpu.SMEM),
        pl.BlockSpec((1, tt, D_MODEL), lambda b, t: (b, t, 0)),
        pl.BlockSpec((1, D_MODEL), const2),
        pl.BlockSpec((None, D_MODEL, D_IN), layer3),
        pl.BlockSpec((None, D_MODEL, D_MODEL), layer3),
        pl.BlockSpec((CONV_W, GROUP_W), const2),
        pl.BlockSpec((None, S5_NG, S5_FOLD, S5_FOLD + S5_RI), layer4),
        pl.BlockSpec((None, S5_NG, S5_RI, S5_FOLD), layer4),
        pl.BlockSpec((3, S5_NG, S5_RI), const3),
        pl.BlockSpec((1, GROUP_W), const2),
        pl.BlockSpec((None, GROUP_W, 2 * GROUP_W), layer3),
        pl.BlockSpec((1, ATT_QH * ATT_HD), const2),
        pl.BlockSpec((1, ATT_KVH * ATT_HD), const2),
        pl.BlockSpec((1, GROUP_W), const2),
        pl.BlockSpec((1, GROUP_W), const2),
        pl.BlockSpec((1, D_MODEL), const2),
    ]
    scratch = [
        pltpu.VMEM((N_CHUNKS, tt, LANES), F32),
        pltpu.VMEM((8 + tt, GROUP_W), F32),
        pltpu.VMEM((WIN_CHUNKS * CHUNK + tt, LANES), BF16),
        pltpu.VMEM((WIN_CHUNKS * CHUNK + tt, LANES), BF16),
        pltpu.VMEM((HG_DK, GROUP_W), F32),
        pltpu.VMEM((tt, GROUP_W), F32),
        pltpu.VMEM((2, tt, LANES), F32),
        pltpu.VMEM((2, tt, LANES), F32),
        pltpu.VMEM((2, tt, LANES), F32),
        pltpu.VMEM((2, S5_NG, LANES), F32),
        pltpu.VMEM((nb * S5_NG, LANES), F32),
        pltpu.VMEM((nb * S5_NG, LANES), F32),
        pltpu.VMEM((nb * S5_NG, LANES), F32),
        pltpu.VMEM((S5_NG, nb, GROUP_W), F32),
        pltpu.VMEM((2, tt, LANES), F32),
        pltpu.VMEM((tt, GROUP_W), F32),
        pltpu.VMEM((tt, D_MODEL), BF16),
    ]
    return pl.pallas_call(
        functools.partial(_mixer_kernel, tt=tt),
        grid=(bsz, seq // tt),
        in_specs=in_specs,
        out_specs=pl.BlockSpec((1, tt, D_MODEL), lambda b, t: (b, t, 0)),
        out_shape=jax.ShapeDtypeStruct(x.shape, x.dtype),
        scratch_shapes=scratch,
        compiler_params=pltpu.CompilerParams(
            dimension_semantics=("parallel", "arbitrary"),
            vmem_limit_bytes=VMEM_LIMIT_BYTES),
        name="mixer_layer",
    )(sinks, x, nm, w_in, w_out, conv_w, w1, cexp, pq, dskip, wglu, qg, kg, lb, hgn, gn)


def _ffn_kernel(x_ref, nf_ref, w1_ref, w2_ref, o_ref, acc_sc):
    x = x_ref[...]
    ms = jnp.mean(x * x, axis=-1, keepdims=True)
    hn = (x * lax.rsqrt(ms + EPS) * nf_ref[...]).astype(BF16)
    for j in range(D_FF // FFN_COLS):
        a = jnp.maximum(_dot(hn, w1_ref[:, j * FFN_COLS:(j + 1) * FFN_COLS]), 0.0)
        part = _dot((a * a).astype(BF16), w2_ref[j * FFN_COLS:(j + 1) * FFN_COLS, :])
        if j == 0:
            acc_sc[...] = x + part
        else:
            acc_sc[...] += part
    o_ref[...] = acc_sc[...]


def _ffn_layer(layer, x2, nf, w1, w2):
    rows = x2.shape[0]
    tm = min(FFN_TILE, rows)
    assert rows % tm == 0
    return pl.pallas_call(
        _ffn_kernel,
        grid=(rows // tm,),
        in_specs=[
            pl.BlockSpec((tm, D_MODEL), lambda i: (i, 0)),
            pl.BlockSpec((1, D_MODEL), lambda i: (0, 0)),
            pl.BlockSpec((None, D_MODEL, D_FF), lambda i: (layer, 0, 0), pipeline_mode=pl.Buffered(1)),
            pl.BlockSpec((None, D_FF, D_MODEL), lambda i: (layer, 0, 0), pipeline_mode=pl.Buffered(1)),
        ],
        out_specs=pl.BlockSpec((tm, D_MODEL), lambda i: (i, 0)),
        out_shape=jax.ShapeDtypeStruct(x2.shape, x2.dtype),
        scratch_shapes=[pltpu.VMEM((tm, D_MODEL), F32)],
        compiler_params=pltpu.CompilerParams(
            dimension_semantics=("parallel",),
            vmem_limit_bytes=VMEM_LIMIT_BYTES),
        name="ffn_layer",
    )(x2, nf, w1, w2)


def _s5_tables(lam_re, lam_im, b_re, b_im, c_re, c_im, log_dt):
    hp = lax.Precision.HIGHEST
    lr = jnp.minimum(lam_re.astype(F32), -1e-4)
    li = lam_im.astype(F32)
    dt = jnp.exp(log_dt.astype(F32))[:, None]
    mag = jnp.exp(lr * dt)
    ar = mag * jnp.cos(li * dt)
    ai = mag * jnp.sin(li * dt)
    den = lr * lr + li * li
    zr = ((ar - 1.0) * lr + ai * li) / den
    zi = (ai * lr - (ar - 1.0) * li) / den
    bre = b_re.astype(F32)
    bim = b_im.astype(F32)
    bbr = zr[..., None] * bre - zi[..., None] * bim
    bbi = zr[..., None] * bim + zi[..., None] * bre
    d = jnp.arange(S5_BLOCK + 1, dtype=F32)[:, None, None]
    magd = jnp.exp(d * (lr * dt))
    pr = magd * jnp.cos(d * (li * dt))
    pi = magd * jnp.sin(d * (li * dt))
    cre = c_re.astype(F32)[None]
    cim = c_im.astype(F32)[None]
    car = cre * pr[:, :, None, :] - cim * pi[:, :, None, :]
    cai = cre * pi[:, :, None, :] + cim * pr[:, :, None, :]
    c_a = jnp.concatenate([car, -cai], axis=3)[:S5_BLOCK].transpose(1, 0, 2, 3)
    kern = jnp.einsum('gmp,gpj->gmj', c_a.reshape(S5_NG, S5_FOLD, S5_RI),
                      jnp.concatenate([bbr, bbi], axis=1), precision=hp)
    kcat = jnp.swapaxes(kern, 1, 2)
    lane = jnp.arange(S5_FOLD)
    toep = jnp.stack([jnp.where(lane >= S5_GROUP * k, jnp.roll(kcat, S5_GROUP * k, axis=2), 0.0)
                      for k in range(S5_BLOCK)], axis=1)
    toep = toep.reshape(S5_NG, S5_FOLD, S5_FOLD)
    prk = jnp.flip(pr[:S5_BLOCK], axis=0).transpose(1, 0, 2)[:, :, None, :]
    pik = jnp.flip(pi[:S5_BLOCK], axis=0).transpose(1, 0, 2)[:, :, None, :]
    bbr_t = bbr.transpose(0, 2, 1)[:, None]
    bbi_t = bbi.transpose(0, 2, 1)[:, None]
    s_in = jnp.concatenate([prk * bbr_t - pik * bbi_t, prk * bbi_t + pik * bbr_t], axis=3)
    s_in = s_in.reshape(S5_NG, S5_FOLD, S5_RI)
    w1 = jnp.concatenate([toep, s_in], axis=2).astype(BF16)
    s_out = jnp.concatenate([car[1:], -cai[1:]], axis=3).transpose(1, 0, 2, 3)
    cexp = jnp.swapaxes(s_out.reshape(S5_NG, S5_FOLD, S5_RI), 1, 2).astype(BF16)
    p_rot = jnp.concatenate([pr[S5_BLOCK], pr[S5_BLOCK]], axis=-1)
    q_rot = jnp.concatenate([-pi[S5_BLOCK], pi[S5_BLOCK]], axis=-1)
    pq = jnp.stack([p_rot, q_rot, -q_rot])
    return w1, cexp, pq


def kernel(x, w_in, w_out, norm_mix, norm_ffn, conv_w, s5_lam_re, s5_lam_im, s5_b_re, s5_b_im, s5_c_re, s5_c_im, s5_d, s5_log_dt, s5_w_glu, attn_q_norm, attn_k_norm, attn_sinks, hg_lower_bounds, hg_out_norm, group_norm, w_ff1, w_ff2):
    bsz, seq, _ = x.shape
    depth = w_in.shape[0]
    lb_p = jax.nn.softmax(hg_lower_bounds.astype(F32), axis=0)
    lb_table = jnp.cumsum(lb_p, axis=0) - lb_p[0]
    def swap_mid_heads(a, base, axis):
        lo, mid, hi = base + ATT_HD, base + 2 * ATT_HD, base + 3 * ATT_HD
        h1 = lax.slice_in_dim(a, lo, mid, axis=axis)
        h2 = lax.slice_in_dim(a, mid, hi, axis=axis)
        a = lax.dynamic_update_slice_in_dim(a, h2, lo, axis=axis)
        return lax.dynamic_update_slice_in_dim(a, h1, mid, axis=axis)

    w1, cexp, pq = jax.vmap(_s5_tables)(s5_lam_re, s5_lam_im, s5_b_re, s5_b_im, s5_c_re, s5_c_im,
                                         s5_log_dt)
    w_in_b = swap_mid_heads(w_in.astype(BF16), J_AQ * LANES, 2)
    w_out_b = swap_mid_heads(w_out.astype(BF16), 2 * GROUP_W, 1)
    gn_b = swap_mid_heads(group_norm, 2 * GROUP_W, 1)
    w_glu_b = s5_w_glu.astype(BF16)
    w_ff1_b = w_ff1.astype(BF16)
    w_ff2_b = w_ff2.astype(BF16)
    qg = jnp.tile(attn_q_norm, (1, ATT_QH))
    kg = jnp.tile(attn_k_norm, (1, ATT_KVH))
    hgn = jnp.tile(hg_out_norm, (1, HG_HEADS))
    row = lambda a, l: a[l].reshape(1, -1)
    for l in range(depth):
        x = _mixer_layer(
            l, x, attn_sinks[l].astype(F32), row(norm_mix, l), w_in_b, w_out_b,
            conv_w[l].astype(F32), w1, cexp, pq[l], row(s5_d, l), w_glu_b,
            row(qg, l), row(kg, l), row(lb_table, l), row(hgn, l), row(gn_b, l))
        x = _ffn_layer(l, x.reshape(bsz * seq, D_MODEL), row(norm_ffn, l),
                       w_ff1_b, w_ff2_b).reshape(bsz, seq, D_MODEL)
    return x
```

```python
import functools

import jax
import jax.numpy as jnp
from jax import lax
from jax.experimental import pallas as pl
from jax.experimental.pallas import tpu as pltpu

F32 = jnp.float32
BF16 = jnp.bfloat16

D_MODEL = 1024
GROUP_W = 256
N_GROUPS = 4
CHUNK = 64
CONV_W = 3
S5_GROUP = 16
S5_NG = 16
S5_STATE = 64
S5_BLOCK = 16
S5_FOLD = S5_BLOCK * S5_GROUP
S5_RI = 2 * S5_STATE
S5_PITCH = 24
ATT_HD = 64
ATT_QH = 4
ATT_KVH = 2
ATT_REP = 2
ATT_SCALE = ATT_HD ** -0.5
WIN_CHUNKS = 2
HG_HEADS = 4
HG_DK = 64
HG_BLOCK = 16
HG_PITCH = 24
D_FF = 4096
D_IN = 2560
EPS = 1e-6
LANES = 128
MXU_COLS = 256

J_CVH, J_CVB, J_CVC, J_S5U = 0, 2, 4, 6
J_AQ, J_AK, J_AV = 8, 10, 11
J_HQ, J_HF, J_HI, J_HG = 12, 14, 16, 18
N_CHUNKS = D_IN // LANES

SEQ_TILE = 512
FFN_TILE = 1024
FFN_COLS = 1024
VMEM_LIMIT_BYTES = 56 * 1024 * 1024


def _dot(a, b):
    return jnp.dot(a, b, preferred_element_type=F32)


def _dot_nt(a, b):
    return lax.dot_general(a, b, (((1,), (1,)), ((), ())), preferred_element_type=F32)


def _dot_tn(a, b):
    return lax.dot_general(a, b, (((0,), (0,)), ((), ())), preferred_element_type=F32)


def _segment_ones(n, seg):
    r = lax.broadcasted_iota(jnp.int32, (n, n), 0) // seg
    c = lax.broadcasted_iota(jnp.int32, (n, n), 1) // seg
    return (r == c).astype(BF16)


def _chunk_transpose16(arrs):
    rows = arrs[0].shape[0]
    chunk = lax.broadcasted_iota(jnp.int32, (rows, S5_FOLD), 1) // S5_GROUP
    for j in range(4):
        s = 1 << j
        low = ((chunk >> j) & 1) == 0
        new = list(arrs)
        for i in range(16):
            if i & s:
                continue
            a, b = arrs[i], arrs[i + s]
            new[i] = jnp.where(low, a, pltpu.roll(b, S5_GROUP * s, axis=1))
            new[i + s] = jnp.where(low, pltpu.roll(a, S5_FOLD - S5_GROUP * s, axis=1), b)
        arrs = new
    return arrs


def _halves(ref, base):
    return jnp.concatenate([ref[base], ref[base + 1]], axis=1)


def _mixer_kernel(sinks_ref, x_ref, nm_ref, win_ref, wout_ref, convw_ref, w1_ref, cexp_ref,
                  pq_ref, dskip_ref, wglu_ref, qg_ref, kg_ref, lb_ref, hgn_ref, gn_ref, perm_ref,
                  o_ref,
                  proj_sc, hgp_sc, zp_sc, kext_sc, vext_sc, hgs_sc, hgi_sc,
                  s5s_sc, s5c_sc, s5i_sc, s5y_sc, s5o_sc, att_sc, ycat_sc,
                  *, tt):
    t = pl.program_id(1)
    nb = tt // S5_BLOCK
    nhb = tt // HG_BLOCK
    ncp = tt // CHUNK
    band = (WIN_CHUNKS + 1) * CHUNK
    hist = WIN_CHUNKS * CHUNK

    @pl.when(t == 0)
    def _reset():
        zp_sc[0:8, :] = jnp.zeros((8, GROUP_W), F32)
        kext_sc[0:hist, :] = jnp.zeros((hist, LANES), BF16)
        vext_sc[0:hist, :] = jnp.zeros((hist, LANES), BF16)
        hgs_sc[...] = jnp.zeros((HG_DK, GROUP_W), F32)
        s5s_sc[...] = jnp.zeros((2, S5_NG, LANES), F32)

    x = x_ref[0]
    ms = jnp.mean(x * x, axis=-1, keepdims=True)
    hn = (x * lax.rsqrt(ms + EPS) * nm_ref[...]).astype(BF16)

    half_rows = tt // 2
    pieces = [(j * LANES, r0)
              for j in (J_S5U, J_HQ, J_HF, J_HI, J_HG, J_AQ, J_AK, J_CVH, J_CVB, J_CVC)
              for r0 in (0, half_rows)]
    n_s5, n_hg = 2, 10
    emitted = [0]

    def emit(n=1):
        for _ in range(n):
            if emitted[0] < len(pieces):
                c0, r0 = pieces[emitted[0]]
                emitted[0] += 1
                res = _dot(hn[r0:r0 + half_rows, :], win_ref[:, c0:c0 + MXU_COLS])
                for j in range(MXU_COLS // LANES):
                    page = c0 // LANES + j
                    if J_HQ <= page < J_HI:
                        for blk in range(half_rows // HG_BLOCK):
                            dst = (r0 // HG_BLOCK + blk) * HG_PITCH
                            hgp_sc[page - J_HQ, dst:dst + HG_BLOCK, :] = (
                                res[blk * HG_BLOCK:(blk + 1) * HG_BLOCK, j * LANES:(j + 1) * LANES])
                    else:
                        proj_sc[page, r0:r0 + half_rows, :] = res[:, j * LANES:(j + 1) * LANES]

    def emit_until(k):
        emit(k - emitted[0])

    emit_until(n_s5)

    def group_norm(y, g):
        gain = gn_ref[:, g * GROUP_W:(g + 1) * GROUP_W]
        msq = jnp.mean(y * y, axis=-1, keepdims=True)
        return y * lax.rsqrt(msq + EPS) * gain

    ones256 = _segment_ones(GROUP_W, HG_DK)
    ones128 = _segment_ones(LANES, ATT_HD)

    folded = [jnp.concatenate([proj_sc[J_S5U + half, pl.ds(k, nb, stride=S5_BLOCK), :]
                               for half in range(2)], axis=1)
              for k in range(S5_BLOCK)]
    emit()
    per_group = _chunk_transpose16(folded)
    emit()
    for g in range(S5_NG):
        zg = _dot(per_group[g].astype(BF16), w1_ref[g])
        s5y_sc[g] = zg[:, 0:S5_FOLD]
        s5c_sc[pl.ds(g, nb, stride=S5_PITCH), :] = zg[:, S5_FOLD:S5_FOLD + S5_RI]
        if g % 8 == 7:
            emit()
    p_rot, q_rot, q_swp = pq_ref[0], pq_ref[1], pq_ref[2]
    st, sw = s5s_sc[0], s5s_sc[1]
    for b in range(nb):
        s5i_sc[b * S5_PITCH:b * S5_PITCH + S5_NG, :] = st
        c_b = s5c_sc[b * S5_PITCH:b * S5_PITCH + S5_NG, :]
        w_b = pltpu.roll(c_b, S5_STATE, axis=1)
        st, sw = (p_rot * st + q_rot * sw + c_b, p_rot * sw + q_swp * st + w_b)
    s5s_sc[0] = st
    s5s_sc[1] = sw
    emit()
    outs = []
    for g in range(S5_NG):
        s_in = s5i_sc[pl.ds(g, nb, stride=S5_PITCH), :]
        outs.append(s5y_sc[g] + _dot(s_in.astype(BF16), cexp_ref[g]))
    emit()
    unfolded = _chunk_transpose16(outs)
    emit()
    for k in range(S5_BLOCK):
        for half in range(2):
            s5o_sc[half, pl.ds(k, nb, stride=S5_BLOCK), :] = (
                unfolded[k][:, LANES * half:LANES * (half + 1)])
    y5 = _halves(s5o_sc, 0) + dskip_ref[...] * _halves(proj_sc, J_S5U)
    y5 = jax.nn.gelu(y5)
    emit_until(n_hg)
    glu = _dot(y5.astype(BF16), wglu_ref[...])
    y_b = glu[:, 0:GROUP_W] * jax.nn.sigmoid(glu[:, GROUP_W:2 * GROUP_W])
    ycat_sc[:, GROUP_W:2 * GROUP_W] = group_norm(y_b, 1).astype(BF16)

    def pos_major(src_sc, j, pitch):
        return jnp.concatenate(
            [jnp.concatenate([src_sc[j + half, pl.ds(p, nhb, stride=pitch), :]
                              for p in range(HG_BLOCK)], axis=0)
             for half in range(2)], axis=1)

    hq = pos_major(hgp_sc, 0, HG_PITCH)
    vv = pos_major(proj_sc, J_HI, HG_BLOCK)
    lb = lb_ref[...]
    fg = lb + (1.0 - lb) * jax.nn.sigmoid(pos_major(hgp_sc, J_HF - J_HQ, HG_PITCH))
    kk = 1.0 - fg
    lf = jnp.log(fg)
    cum = [lf[0:nhb, :]]
    for p in range(1, HG_BLOCK):
        cum.append(cum[-1] + lf[p * nhb:(p + 1) * nhb, :])
    bc = jnp.concatenate(cum, axis=0)
    b_last = cum[-1]
    gk = jnp.log(kk) - bc
    o_groups = []
    for p in range(HG_BLOCK):
        m = (p + 1) * nhb
        q_p = hq[p * nhb:m, :][None]
        b_p = bc[p * nhb:m, :][None]
        e = q_p * jnp.exp(b_p + gk[0:m, :].reshape(p + 1, nhb, GROUP_W))
        sc = _dot(e.reshape(m, GROUP_W).astype(BF16), ones256) * vv[0:m, :]
        o_groups.append(jnp.sum(sc.reshape(p + 1, nhb, GROUP_W), axis=0))
        if p >= 6:
            emit()
    o_pm = jnp.concatenate(o_groups, axis=0)
    emit_until(len(pieces))
    perm = perm_ref[...]
    qd = _dot(perm, (hq * jnp.exp(bc)).astype(BF16)).astype(BF16)
    k_end = kk * jnp.exp(jnp.concatenate([b_last] * HG_BLOCK, axis=0) - bc)
    k_end = _dot(perm, k_end.astype(BF16)).astype(BF16)
    o_hi = o_pm.astype(BF16)
    o_lo = (o_pm - o_hi.astype(F32)).astype(BF16)
    o_intra = _dot(perm, o_hi) + _dot(perm, o_lo)
    vv_n = _halves(proj_sc, J_HI).astype(BF16)
    lane_head = lax.broadcasted_iota(jnp.int32, (HG_BLOCK, GROUP_W), 1) // HG_DK
    head_masks = [lane_head == h for h in range(HG_HEADS)]
    dec = jnp.exp(b_last)

    def per_head_rows(a):
        return jnp.concatenate([jnp.where(m, a, jnp.zeros_like(a)) for m in head_masks], axis=0)

    upds = []
    for n in range(nhb):
        rows = slice(n * HG_BLOCK, (n + 1) * HG_BLOCK)
        v_n = vv_n[rows, :]
        v_stk = jnp.concatenate([v_n[:, h * HG_DK:(h + 1) * HG_DK] for h in range(HG_HEADS)], axis=0)
        upds.append(_dot_tn(v_stk, per_head_rows(k_end[rows, :])))
    s_val = hgs_sc[...]
    states = []
    for n in range(nhb):
        states.append(s_val.astype(BF16))
        s_val = s_val * dec[n:n + 1, :] + upds[n]
    hgs_sc[...] = s_val
    for n in range(nhb):
        rows = slice(n * HG_BLOCK, (n + 1) * HG_BLOCK)
        out = _dot_nt(per_head_rows(qd[rows, :]), states[n])
        hgi_sc[rows, :] = jnp.concatenate(
            [out[h * HG_BLOCK:(h + 1) * HG_BLOCK, :] for h in range(HG_HEADS)], axis=1)
    o_h = o_intra + hgi_sc[...]
    o_ms = _dot((o_h * o_h).astype(BF16), ones256) * (1.0 / HG_DK)
    gate = _halves(proj_sc, J_HG)
    y_d = o_h * lax.rsqrt(o_ms + EPS) * hgn_ref[...] * (gate * jax.nn.sigmoid(gate))
    ycat_sc[:, 3 * GROUP_W:4 * GROUP_W] = group_norm(y_d, 3).astype(BF16)

    z = _halves(proj_sc, J_CVC) * _halves(proj_sc, J_CVH)
    zp_sc[8:8 + tt, :] = z
    acc = (convw_ref[0:1, :] * zp_sc[6:6 + tt, :] + convw_ref[1:2, :] * zp_sc[7:7 + tt, :]
           + convw_ref[2:3, :] * z)
    y_a = _halves(proj_sc, J_CVB) * acc
    zp_sc[0:8, :] = zp_sc[tt:tt + 8, :]
    ycat_sc[:, 0:GROUP_W] = group_norm(y_a, 0).astype(BF16)

    aq = _halves(proj_sc, J_AQ)
    ak = proj_sc[J_AK]
    q_ms = _dot((aq * aq).astype(BF16), ones256) * (1.0 / ATT_HD)
    k_ms = _dot((ak * ak).astype(BF16), ones128) * (1.0 / ATT_HD)
    qn = aq * lax.rsqrt(q_ms + EPS) * qg_ref[...] * ATT_SCALE
    kn = ak * lax.rsqrt(k_ms + EPS) * kg_ref[...]
    kext_sc[hist:hist + tt, :] = kn.astype(BF16)
    vext_sc[hist:hist + tt, :] = proj_sc[J_AV].astype(BF16)
    lane_kv = lax.broadcasted_iota(jnp.int32, (tt, LANES), 1) // ATT_HD
    q_masked = [[jnp.where(lane_kv == g, qn[:, r * LANES:(r + 1) * LANES], 0.0).astype(BF16)
                 for r in range(ATT_REP)] for g in range(ATT_KVH)]
    col = lax.broadcasted_iota(jnp.int32, (ATT_REP * CHUNK, band), 1)
    row = lax.broadcasted_iota(jnp.int32, (ATT_REP * CHUNK, 1), 0)
    first_kv = lax.broadcasted_iota(jnp.int32, (CHUNK, LANES), 1) < ATT_HD
    blocks = [(c, g) for c in range(ncp) for g in range(ATT_KVH)]
    scores = []
    for c, g in blocks:
        q2 = jnp.concatenate([q_masked[g][r][c * CHUNK:(c + 1) * CHUNK, :] for r in range(ATT_REP)],
                             axis=0)
        scores.append(_dot_nt(q2, kext_sc[c * CHUNK:c * CHUNK + band, :]))
    probs, dens = [], []
    for (c, g), s in zip(blocks, scores):
        if c < WIN_CHUNKS:
            first_valid = jnp.maximum(WIN_CHUNKS - (t * ncp + c), 0) * CHUNK
            s = jnp.where(col >= first_valid, s, -jnp.inf)
        sink = jnp.where(row < CHUNK, sinks_ref[ATT_REP * g], sinks_ref[ATT_REP * g + 1])
        m = jnp.maximum(jnp.max(s, axis=-1, keepdims=True), sink)
        p = jnp.exp(s - m)
        dens.append(jnp.sum(p, axis=-1, keepdims=True) + jnp.exp(sink - m))
        probs.append(p.astype(BF16))
    outs_kv = {}
    for (c, g), p, den in zip(blocks, probs, dens):
        outs_kv[c, g] = _dot(p, vext_sc[c * CHUNK:c * CHUNK + band, :]) / den
    for c in range(ncp):
        for r in range(ATT_REP):
            att_sc[c * CHUNK:(c + 1) * CHUNK, r * LANES:(r + 1) * LANES] = jnp.where(
                first_kv, outs_kv[c, 0][r * CHUNK:(r + 1) * CHUNK, :],
                outs_kv[c, 1][r * CHUNK:(r + 1) * CHUNK, :])
    kext_sc[0:hist, :] = kext_sc[tt:tt + hist, :]
    vext_sc[0:hist, :] = vext_sc[tt:tt + hist, :]
    ycat_sc[:, 2 * GROUP_W:3 * GROUP_W] = group_norm(att_sc[...], 2).astype(BF16)

    o_ref[0] = x + _dot(ycat_sc[...], wout_ref[...])


def _mixer_layer(layer, x, sinks, nm, w_in, w_out, conv_w, w1, cexp, pq, dskip, wglu, qg, kg, lb, hgn, gn):
    bsz, seq, _ = x.shape
    tt = min(SEQ_TILE, seq)
    assert seq % tt == 0 and tt % CHUNK == 0 and tt >= WIN_CHUNKS * CHUNK
    nb = tt // S5_BLOCK
    assert nb % 8 == 0
    nat = jnp.arange(tt)
    perm = (jnp.arange(tt)[None, :] == ((nat % HG_BLOCK) * (tt // HG_BLOCK) + nat // HG_BLOCK)[:, None]).astype(BF16)
    const2 = lambda b, t: (0, 0)
    const3 = lambda b, t: (0, 0, 0)
    layer3 = lambda b, t: (layer, 0, 0)
    layer4 = lambda b, t: (layer, 0, 0, 0)
    in_specs = [
        pl.BlockSpec(memory_space=pltpu.SMEM),
        pl.BlockSpec((1, tt, D_MODEL), lambda b, t: (b, t, 0)),
        pl.BlockSpec((1, D_MODEL), const2),
        pl.BlockSpec((None, D_MODEL, D_IN), layer3),
        pl.BlockSpec((None, D_MODEL, D_MODEL), layer3),
        pl.BlockSpec((CONV_W, GROUP_W), const2),
        pl.BlockSpec((None, S5_NG, S5_FOLD, S5_FOLD + S5_RI), layer4),
        pl.BlockSpec((None, S5_NG, S5_RI, S5_FOLD), layer4),
        pl.BlockSpec((3, S5_NG, S5_RI), const3),
        pl.BlockSpec((1, GROUP_W), const2),
        pl.BlockSpec((None, GROUP_W, 2 * GROUP_W), layer3),
        pl.BlockSpec((1, ATT_QH * ATT_HD), const2),
        pl.BlockSpec((1, ATT_KVH * ATT_HD), const2),
        pl.BlockSpec((1, GROUP_W), const2),
        pl.BlockSpec((1, GROUP_W), const2),
        pl.BlockSpec((1, D_MODEL), const2),
        pl.BlockSpec((tt, tt), const2),
    ]
    scratch = [
        pltpu.VMEM((N_CHUNKS, tt, LANES), F32),
        pltpu.VMEM((J_HI - J_HQ, tt * HG_PITCH // HG_BLOCK, LANES), F32),
        pltpu.VMEM((8 + tt, GROUP_W), F32),
        pltpu.VMEM((WIN_CHUNKS * CHUNK + tt, LANES), BF16),
        pltpu.VMEM((WIN_CHUNKS * CHUNK + tt, LANES), BF16),
        pltpu.VMEM((HG_DK, GROUP_W), F32),
        pltpu.VMEM((tt, GROUP_W), F32),
        pltpu.VMEM((2, S5_NG, LANES), F32),
        pltpu.VMEM((nb * S5_PITCH, LANES), F32),
        pltpu.VMEM((nb * S5_PITCH, LANES), F32),
        pltpu.VMEM((S5_NG, nb, GROUP_W), F32),
        pltpu.VMEM((2, tt, LANES), F32),
        pltpu.VMEM((tt, GROUP_W), F32),
        pltpu.VMEM((tt, D_MODEL), BF16),
    ]
    return pl.pallas_call(
        functools.partial(_mixer_kernel, tt=tt),
        grid=(bsz, seq // tt),
        in_specs=in_specs,
        out_specs=pl.BlockSpec((1, tt, D_MODEL), lambda b, t: (b, t, 0)),
        out_shape=jax.ShapeDtypeStruct(x.shape, x.dtype),
        scratch_shapes=scratch,
        compiler_params=pltpu.CompilerParams(
            dimension_semantics=("parallel", "arbitrary"),
            vmem_limit_bytes=VMEM_LIMIT_BYTES),
        name="mixer_layer",
    )(sinks, x, nm, w_in, w_out, conv_w, w1, cexp, pq, dskip, wglu, qg, kg, lb, hgn, gn, perm)


def _ffn_kernel(x_ref, nf_ref, w1_ref, w2_ref, o_ref, acc_sc):
    x = x_ref[...]
    ms = jnp.mean(x * x, axis=-1, keepdims=True)
    hn = (x * lax.rsqrt(ms + EPS) * nf_ref[...]).astype(BF16)
    for j in range(D_FF // FFN_COLS):
        a = jnp.maximum(_dot(hn, w1_ref[:, j * FFN_COLS:(j + 1) * FFN_COLS]), 0.0)
        part = _dot((a * a).astype(BF16), w2_ref[j * FFN_COLS:(j + 1) * FFN_COLS, :])
        if j == 0:
            acc_sc[...] = x + part
        else:
            acc_sc[...] += part
    o_ref[...] = acc_sc[...]


def _ffn_layer(layer, x2, nf, w1, w2):
    rows = x2.shape[0]
    tm = min(FFN_TILE, rows)
    assert rows % tm == 0
    return pl.pallas_call(
        _ffn_kernel,
        grid=(rows // tm,),
        in_specs=[
            pl.BlockSpec((tm, D_MODEL), lambda i: (i, 0)),
            pl.BlockSpec((1, D_MODEL), lambda i: (0, 0)),
            pl.BlockSpec((None, D_MODEL, D_FF), lambda i: (layer, 0, 0), pipeline_mode=pl.Buffered(1)),
            pl.BlockSpec((None, D_FF, D_MODEL), lambda i: (layer, 0, 0), pipeline_mode=pl.Buffered(1)),
        ],
        out_specs=pl.BlockSpec((tm, D_MODEL), lambda i: (i, 0)),
        out_shape=jax.ShapeDtypeStruct(x2.shape, x2.dtype),
        scratch_shapes=[pltpu.VMEM((tm, D_MODEL), F32)],
        compiler_params=pltpu.CompilerParams(
            dimension_semantics=("parallel",),
            vmem_limit_bytes=VMEM_LIMIT_BYTES),
        name="ffn_layer",
    )(x2, nf, w1, w2)


def _s5_tables(lam_re, lam_im, b_re, b_im, c_re, c_im, log_dt):
    hp = lax.Precision.HIGHEST
    lr = jnp.minimum(lam_re.astype(F32), -1e-4)
    li = lam_im.astype(F32)
    dt = jnp.exp(log_dt.astype(F32))[:, None]
    mag = jnp.exp(lr * dt)
    ar = mag * jnp.cos(li * dt)
    ai = mag * jnp.sin(li * dt)
    den = lr * lr + li * li
    zr = ((ar - 1.0) * lr + ai * li) / den
    zi = (ai * lr - (ar - 1.0) * li) / den
    bre = b_re.astype(F32)
    bim = b_im.astype(F32)
    bbr = zr[..., None] * bre - zi[..., None] * bim
    bbi = zr[..., None] * bim + zi[..., None] * bre
    d = jnp.arange(S5_BLOCK + 1, dtype=F32)[:, None, None]
    magd = jnp.exp(d * (lr * dt))
    pr = magd * jnp.cos(d * (li * dt))
    pi = magd * jnp.sin(d * (li * dt))
    cre = c_re.astype(F32)[None]
    cim = c_im.astype(F32)[None]
    car = cre * pr[:, :, None, :] - cim * pi[:, :, None, :]
    cai = cre * pi[:, :, None, :] + cim * pr[:, :, None, :]
    c_a = jnp.concatenate([car, -cai], axis=3)[:S5_BLOCK].transpose(1, 0, 2, 3)
    kern = jnp.einsum('gmp,gpj->gmj', c_a.reshape(S5_NG, S5_FOLD, S5_RI),
                      jnp.concatenate([bbr, bbi], axis=1), precision=hp)
    kcat = jnp.swapaxes(kern, 1, 2)
    lane = jnp.arange(S5_FOLD)
    toep = jnp.stack([jnp.where(lane >= S5_GROUP * k, jnp.roll(kcat, S5_GROUP * k, axis=2), 0.0)
                      for k in range(S5_BLOCK)], axis=1)
    toep = toep.reshape(S5_NG, S5_FOLD, S5_FOLD)
    prk = jnp.flip(pr[:S5_BLOCK], axis=0).transpose(1, 0, 2)[:, :, None, :]
    pik = jnp.flip(pi[:S5_BLOCK], axis=0).transpose(1, 0, 2)[:, :, None, :]
    bbr_t = bbr.transpose(0, 2, 1)[:, None]
    bbi_t = bbi.transpose(0, 2, 1)[:, None]
    s_in = jnp.concatenate([prk * bbr_t - pik * bbi_t, prk * bbi_t + pik * bbr_t], axis=3)
    s_in = s_in.reshape(S5_NG, S5_FOLD, S5_RI)
    w1 = jnp.concatenate([toep, s_in], axis=2).astype(BF16)
    s_out = jnp.concatenate([car[1:], -cai[1:]], axis=3).transpose(1, 0, 2, 3)
    cexp = jnp.swapaxes(s_out.reshape(S5_NG, S5_FOLD, S5_RI), 1, 2).astype(BF16)
    p_rot = jnp.concatenate([pr[S5_BLOCK], pr[S5_BLOCK]], axis=-1)
    q_rot = jnp.concatenate([-pi[S5_BLOCK], pi[S5_BLOCK]], axis=-1)
    pq = jnp.stack([p_rot, q_rot, -q_rot])
    return w1, cexp, pq


def kernel(x, w_in, w_out, norm_mix, norm_ffn, conv_w, s5_lam_re, s5_lam_im, s5_b_re, s5_b_im, s5_c_re, s5_c_im, s5_d, s5_log_dt, s5_w_glu, attn_q_norm, attn_k_norm, attn_sinks, hg_lower_bounds, hg_out_norm, group_norm, w_ff1, w_ff2):
    bsz, seq, _ = x.shape
    depth = w_in.shape[0]
    lb_p = jax.nn.softmax(hg_lower_bounds.astype(F32), axis=0)
    lb_table = jnp.cumsum(lb_p, axis=0) - lb_p[0]
    def swap_mid_heads(a, base, axis):
        lo, mid, hi = base + ATT_HD, base + 2 * ATT_HD, base + 3 * ATT_HD
        h1 = lax.slice_in_dim(a, lo, mid, axis=axis)
        h2 = lax.slice_in_dim(a, mid, hi, axis=axis)
        a = lax.dynamic_update_slice_in_dim(a, h2, lo, axis=axis)
        return lax.dynamic_update_slice_in_dim(a, h1, mid, axis=axis)

    w1, cexp, pq = jax.vmap(_s5_tables)(s5_lam_re, s5_lam_im, s5_b_re, s5_b_im, s5_c_re, s5_c_im,
                                         s5_log_dt)
    w_in_b = swap_mid_heads(w_in.astype(BF16), J_AQ * LANES, 2)
    w_out_b = swap_mid_heads(w_out.astype(BF16), 2 * GROUP_W, 1)
    gn_b = swap_mid_heads(group_norm, 2 * GROUP_W, 1)
    w_glu_b = s5_w_glu.astype(BF16)
    w_ff1_b = w_ff1.astype(BF16)
    w_ff2_b = w_ff2.astype(BF16)
    qg = jnp.tile(attn_q_norm, (1, ATT_QH))
    kg = jnp.tile(attn_k_norm, (1, ATT_KVH))
    hgn = jnp.tile(hg_out_norm, (1, HG_HEADS))
    row = lambda a, l: a[l].reshape(1, -1)
    for l in range(depth):
        x = _mixer_layer(
            l, x, attn_sinks[l].astype(F32), row(norm_mix, l), w_in_b, w_out_b,
            conv_w[l].astype(F32), w1, cexp, pq[l], row(s5_d, l), w_glu_b,
            row(qg, l), row(kg, l), row(lb_table, l), row(hgn, l), row(gn_b, l))
        x = _ffn_layer(l, x.reshape(bsz * seq, D_MODEL), row(norm_ffn, l),
                       w_ff1_b, w_ff2_b).reshape(bsz, seq, D_MODEL)
    return x
```

```python
import functools

import jax
import jax.numpy as jnp
from jax import lax
from jax.experimental import pallas as pl
from jax.experimental.pallas import tpu as pltpu

F32 = jnp.float32
BF16 = jnp.bfloat16

D_MODEL = 1024
GROUP_W = 256
N_GROUPS = 4
CHUNK = 64
CONV_W = 3
S5_GROUP = 16
S5_NG = 16
S5_STATE = 64
S5_BLOCK = 16
S5_FOLD = S5_BLOCK * S5_GROUP
S5_RI = 2 * S5_STATE
S5_PITCH = 24
ATT_HD = 64
ATT_QH = 4
ATT_KVH = 2
ATT_REP = 2
ATT_SCALE = ATT_HD ** -0.5
WIN_CHUNKS = 2
HG_HEADS = 4
HG_DK = 64
HG_BLOCK = 16
HG_PITCH = 24
D_FF = 4096
D_IN = 2560
EPS = 1e-6
LANES = 128
MXU_COLS = 256

J_CVH, J_CVB, J_CVC, J_S5U = 0, 2, 4, 6
J_AQ, J_AK, J_AV = 8, 10, 11
J_HQ, J_HF, J_HI, J_HG = 12, 14, 16, 18
N_CHUNKS = D_IN // LANES

SEQ_TILE = 512
FFN_TILE = 1024
FFN_COLS = 1024
VMEM_LIMIT_BYTES = 56 * 1024 * 1024


def _dot(a, b):
    return jnp.dot(a, b, preferred_element_type=F32)


def _dot_nt(a, b):
    return lax.dot_general(a, b, (((1,), (1,)), ((), ())), preferred_element_type=F32)


def _dot_tn(a, b):
    return lax.dot_general(a, b, (((0,), (0,)), ((), ())), preferred_element_type=F32)


def _segment_ones(n, seg):
    r = lax.broadcasted_iota(jnp.int32, (n, n), 0) // seg
    c = lax.broadcasted_iota(jnp.int32, (n, n), 1) // seg
    return (r == c).astype(BF16)


def _chunk_transpose16(arrs):
    rows = arrs[0].shape[0]
    chunk = lax.broadcasted_iota(jnp.int32, (rows, S5_FOLD), 1) // S5_GROUP
    for j in range(4):
        s = 1 << j
        low = ((chunk >> j) & 1) == 0
        new = list(arrs)
        for i in range(16):
            if i & s:
                continue
            a, b = arrs[i], arrs[i + s]
            new[i] = jnp.where(low, a, pltpu.roll(b, S5_GROUP * s, axis=1))
            new[i + s] = jnp.where(low, pltpu.roll(a, S5_FOLD - S5_GROUP * s, axis=1), b)
        arrs = new
    return arrs


def _halves(ref, base):
    return jnp.concatenate([ref[base], ref[base + 1]], axis=1)


def _mixer_kernel(sinks_ref, x_ref, nm_ref, win_ref, wout_ref, convw_ref, w1_ref, cexp_ref,
                  pq_ref, dskip_ref, wglu_ref, qg_ref, kg_ref, lb_ref, hgn_ref, gn_ref, perm_ref,
                  o_ref,
                  proj_sc, hgp_sc, s5p_sc, zp_sc, kext_sc, vext_sc, hgs_sc, hgi_sc,
                  s5s_sc, s5c_sc, s5i_sc, s5y_sc, s5o_sc, att_sc, ycat_sc,
                  *, tt):
    t = pl.program_id(1)
    nb = tt // S5_BLOCK
    nhb = tt // HG_BLOCK
    ncp = tt // CHUNK
    band = (WIN_CHUNKS + 1) * CHUNK
    hist = WIN_CHUNKS * CHUNK

    @pl.when(t == 0)
    def _reset():
        zp_sc[0:8, :] = jnp.zeros((8, GROUP_W), F32)
        kext_sc[0:hist, :] = jnp.zeros((hist, LANES), BF16)
        vext_sc[0:hist, :] = jnp.zeros((hist, LANES), BF16)
        hgs_sc[...] = jnp.zeros((HG_DK, GROUP_W), F32)
        s5s_sc[...] = jnp.zeros((2, S5_NG, LANES), F32)

    x = x_ref[0]
    ms = jnp.mean(x * x, axis=-1, keepdims=True)
    hn = (x * lax.rsqrt(ms + EPS) * nm_ref[...]).astype(BF16)

    half_rows = tt // 2
    pieces = [(j * LANES, r0)
              for j in (J_S5U, J_HQ, J_HF, J_HI, J_HG, J_AQ, J_AK, J_CVH, J_CVB, J_CVC)
              for r0 in (0, half_rows)]
    n_s5, n_hg = 2, 10
    emitted = [0]

    def emit(n=1):
        for _ in range(n):
            if emitted[0] < len(pieces):
                c0, r0 = pieces[emitted[0]]
                emitted[0] += 1
                res = _dot(hn[r0:r0 + half_rows, :], win_ref[:, c0:c0 + MXU_COLS])
                for j in range(MXU_COLS // LANES):
                    page = c0 // LANES + j
                    if J_HQ <= page < J_HG:
                        padded = (hgp_sc, page - J_HQ)
                    elif J_S5U <= page < J_AQ:
                        padded = (s5p_sc, page - J_S5U)
                    else:
                        padded = None
                    if padded is not None:
                        for blk in range(half_rows // HG_BLOCK):
                            dst = (r0 // HG_BLOCK + blk) * HG_PITCH
                            padded[0][padded[1], dst:dst + HG_BLOCK, :] = (
                                res[blk * HG_BLOCK:(blk + 1) * HG_BLOCK, j * LANES:(j + 1) * LANES])
                    if not J_HQ <= page < J_HI:
                        proj_sc[page, r0:r0 + half_rows, :] = res[:, j * LANES:(j + 1) * LANES]

    def emit_until(k):
        emit(k - emitted[0])

    emit_until(n_s5)

    def group_norm(y, g):
        gain = gn_ref[:, g * GROUP_W:(g + 1) * GROUP_W]
        msq = jnp.mean(y * y, axis=-1, keepdims=True)
        return y * lax.rsqrt(msq + EPS) * gain

    ones256 = _segment_ones(GROUP_W, HG_DK)
    ones128 = _segment_ones(LANES, ATT_HD)

    folded = [jnp.concatenate([s5p_sc[half, pl.ds(k, nb, stride=HG_PITCH), :]
                               for half in range(2)], axis=1)
              for k in range(S5_BLOCK)]
    emit()
    per_group = _chunk_transpose16(folded)
    emit()
    for g in range(S5_NG):
        zg = _dot(per_group[g].astype(BF16), w1_ref[g])
        s5y_sc[g] = zg[:, 0:S5_FOLD]
        s5c_sc[pl.ds(g, nb, stride=S5_PITCH), :] = zg[:, S5_FOLD:S5_FOLD + S5_RI]
        if g % 8 == 7:
            emit()
    p_rot, q_rot, q_swp = pq_ref[0], pq_ref[1], pq_ref[2]
    st, sw = s5s_sc[0], s5s_sc[1]
    for b in range(nb):
        s5i_sc[b * S5_PITCH:b * S5_PITCH + S5_NG, :] = st
        c_b = s5c_sc[b * S5_PITCH:b * S5_PITCH + S5_NG, :]
        w_b = pltpu.roll(c_b, S5_STATE, axis=1)
        st, sw = (p_rot * st + q_rot * sw + c_b, p_rot * sw + q_swp * st + w_b)
    s5s_sc[0] = st
    s5s_sc[1] = sw
    emit()
    outs = []
    for g in range(S5_NG):
        s_in = s5i_sc[pl.ds(g, nb, stride=S5_PITCH), :]
        outs.append(s5y_sc[g] + _dot(s_in.astype(BF16), cexp_ref[g]))
    emit()
    unfolded = _chunk_transpose16(outs)
    emit()
    for k in range(S5_BLOCK):
        for half in range(2):
            s5o_sc[half, pl.ds(k, nb, stride=S5_BLOCK), :] = (
                unfolded[k][:, LANES * half:LANES * (half + 1)])
    y5 = _halves(s5o_sc, 0) + dskip_ref[...] * _halves(proj_sc, J_S5U)
    y5 = jax.nn.gelu(y5)
    emit_until(n_hg)
    glu = _dot(y5.astype(BF16), wglu_ref[...])
    y_b = glu[:, 0:GROUP_W] * jax.nn.sigmoid(glu[:, GROUP_W:2 * GROUP_W])
    ycat_sc[:, GROUP_W:2 * GROUP_W] = group_norm(y_b, 1).astype(BF16)

    def pos_major(src_sc, j, pitch):
        return jnp.concatenate(
            [jnp.concatenate([src_sc[j + half, pl.ds(p, nhb, stride=pitch), :]
                              for p in range(HG_BLOCK)], axis=0)
             for half in range(2)], axis=1)

    hq = pos_major(hgp_sc, 0, HG_PITCH)
    vv = pos_major(hgp_sc, J_HI - J_HQ, HG_PITCH)
    lb = lb_ref[...]
    fg = lb + (1.0 - lb) * jax.nn.sigmoid(pos_major(hgp_sc, J_HF - J_HQ, HG_PITCH))
    kk = 1.0 - fg
    lf = jnp.log(fg)
    cum = [lf[0:nhb, :]]
    for p in range(1, HG_BLOCK):
        cum.append(cum[-1] + lf[p * nhb:(p + 1) * nhb, :])
    bc = jnp.concatenate(cum, axis=0)
    b_last = cum[-1]
    gk = jnp.log(kk) - bc
    o_groups = []
    for p in range(HG_BLOCK):
        m = (p + 1) * nhb
        q_p = hq[p * nhb:m, :][None]
        b_p = bc[p * nhb:m, :][None]
        e = q_p * jnp.exp(b_p + gk[0:m, :].reshape(p + 1, nhb, GROUP_W))
        sc = _dot(e.reshape(m, GROUP_W).astype(BF16), ones256) * vv[0:m, :]
        o_groups.append(jnp.sum(sc.reshape(p + 1, nhb, GROUP_W), axis=0))
        if p >= 6:
            emit()
    o_pm = jnp.concatenate(o_groups, axis=0)
    emit_until(len(pieces))
    perm = perm_ref[...]
    qd = _dot(perm, (hq * jnp.exp(bc)).astype(BF16)).astype(BF16)
    k_end = kk * jnp.exp(jnp.concatenate([b_last] * HG_BLOCK, axis=0) - bc)
    k_end = _dot(perm, k_end.astype(BF16)).astype(BF16)
    o_hi = o_pm.astype(BF16)
    o_lo = (o_pm - o_hi.astype(F32)).astype(BF16)
    o_intra = _dot(perm, o_hi) + _dot(perm, o_lo)
    vv_n = _halves(proj_sc, J_HI).astype(BF16)
    lane_head = lax.broadcasted_iota(jnp.int32, (HG_BLOCK, GROUP_W), 1) // HG_DK
    head_masks = [lane_head == h for h in range(HG_HEADS)]
    dec = jnp.exp(b_last)

    def per_head_rows(a):
        return jnp.concatenate([jnp.where(m, a, jnp.zeros_like(a)) for m in head_masks], axis=0)

    upds = []
    for n in range(nhb):
        rows = slice(n * HG_BLOCK, (n + 1) * HG_BLOCK)
        v_n = vv_n[rows, :]
        v_stk = jnp.concatenate([v_n[:, h * HG_DK:(h + 1) * HG_DK] for h in range(HG_HEADS)], axis=0)
        upds.append(_dot_tn(v_stk, per_head_rows(k_end[rows, :])))
    s_val = hgs_sc[...]
    states = []
    for n in range(nhb):
        states.append(s_val.astype(BF16))
        s_val = s_val * dec[n:n + 1, :] + upds[n]
    hgs_sc[...] = s_val
    for n in range(nhb):
        rows = slice(n * HG_BLOCK, (n + 1) * HG_BLOCK)
        out = _dot_nt(per_head_rows(qd[rows, :]), states[n])
        hgi_sc[rows, :] = jnp.concatenate(
            [out[h * HG_BLOCK:(h + 1) * HG_BLOCK, :] for h in range(HG_HEADS)], axis=1)
    o_h = o_intra + hgi_sc[...]
    o_ms = _dot((o_h * o_h).astype(BF16), ones256) * (1.0 / HG_DK)
    gate = _halves(proj_sc, J_HG)
    y_d = o_h * lax.rsqrt(o_ms + EPS) * hgn_ref[...] * (gate * jax.nn.sigmoid(gate))
    ycat_sc[:, 3 * GROUP_W:4 * GROUP_W] = group_norm(y_d, 3).astype(BF16)

    z = _halves(proj_sc, J_CVC) * _halves(proj_sc, J_CVH)
    zp_sc[8:8 + tt, :] = z
    acc = (convw_ref[0:1, :] * zp_sc[6:6 + tt, :] + convw_ref[1:2, :] * zp_sc[7:7 + tt, :]
           + convw_ref[2:3, :] * z)
    y_a = _halves(proj_sc, J_CVB) * acc
    zp_sc[0:8, :] = zp_sc[tt:tt + 8, :]
    ycat_sc[:, 0:GROUP_W] = group_norm(y_a, 0).astype(BF16)

    aq = _halves(proj_sc, J_AQ)
    ak = proj_sc[J_AK]
    q_ms = _dot((aq * aq).astype(BF16), ones256) * (1.0 / ATT_HD)
    k_ms = _dot((ak * ak).astype(BF16), ones128) * (1.0 / ATT_HD)
    qn = aq * lax.rsqrt(q_ms + EPS) * qg_ref[...] * ATT_SCALE
    kn = ak * lax.rsqrt(k_ms + EPS) * kg_ref[...]
    kext_sc[hist:hist + tt, :] = kn.astype(BF16)
    vext_sc[hist:hist + tt, :] = proj_sc[J_AV].astype(BF16)
    lane_kv = lax.broadcasted_iota(jnp.int32, (tt, LANES), 1) // ATT_HD
    q_masked = [[jnp.where(lane_kv == g, qn[:, r * LANES:(r + 1) * LANES], 0.0).astype(BF16)
                 for r in range(ATT_REP)] for g in range(ATT_KVH)]
    col = lax.broadcasted_iota(jnp.int32, (ATT_REP * CHUNK, band), 1)
    row = lax.broadcasted_iota(jnp.int32, (ATT_REP * CHUNK, 1), 0)
    first_kv = lax.broadcasted_iota(jnp.int32, (CHUNK, LANES), 1) < ATT_HD
    blocks = [(c, g) for c in range(ncp) for g in range(ATT_KVH)]
    scores = []
    for c, g in blocks:
        q2 = jnp.concatenate([q_masked[g][r][c * CHUNK:(c + 1) * CHUNK, :] for r in range(ATT_REP)],
                             axis=0)
        scores.append(_dot_nt(q2, kext_sc[c * CHUNK:c * CHUNK + band, :]))
    probs, dens = [], []
    for (c, g), s in zip(blocks, scores):
        if c < WIN_CHUNKS:
            first_valid = jnp.maximum(WIN_CHUNKS - (t * ncp + c), 0) * CHUNK
            s = jnp.where(col >= first_valid, s, -jnp.inf)
        sink = jnp.where(row < CHUNK, sinks_ref[ATT_REP * g], sinks_ref[ATT_REP * g + 1])
        m = jnp.maximum(jnp.max(s, axis=-1, keepdims=True), sink)
        p = jnp.exp(s - m)
        dens.append(jnp.sum(p, axis=-1, keepdims=True) + jnp.exp(sink - m))
        probs.append(p.astype(BF16))
    outs_kv = {}
    for (c, g), p, den in zip(blocks, probs, dens):
        outs_kv[c, g] = _dot(p, vext_sc[c * CHUNK:c * CHUNK + band, :]) / den
    for c in range(ncp):
        for r in range(ATT_REP):
            att_sc[c * CHUNK:(c + 1) * CHUNK, r * LANES:(r + 1) * LANES] = jnp.where(
                first_kv, outs_kv[c, 0][r * CHUNK:(r + 1) * CHUNK, :],
                outs_kv[c, 1][r * CHUNK:(r + 1) * CHUNK, :])
    kext_sc[0:hist, :] = kext_sc[tt:tt + hist, :]
    vext_sc[0:hist, :] = vext_sc[tt:tt + hist, :]
    ycat_sc[:, 2 * GROUP_W:3 * GROUP_W] = group_norm(att_sc[...], 2).astype(BF16)

    o_ref[0] = x + _dot(ycat_sc[...], wout_ref[...])


def _mixer_layer(layer, x, sinks, nm, w_in, w_out, conv_w, w1, cexp, pq, dskip, wglu, qg, kg, lb, hgn, gn):
    bsz, seq, _ = x.shape
    tt = min(SEQ_TILE, seq)
    assert seq % tt == 0 and tt % CHUNK == 0 and tt >= WIN_CHUNKS * CHUNK
    nb = tt // S5_BLOCK
    assert nb % 8 == 0
    nat = jnp.arange(tt)
    perm = (jnp.arange(tt)[None, :] == ((nat % HG_BLOCK) * (tt // HG_BLOCK) + nat // HG_BLOCK)[:, None]).astype(BF16)
    const2 = lambda b, t: (0, 0)
    const3 = lambda b, t: (0, 0, 0)
    layer3 = lambda b, t: (layer, 0, 0)
    layer4 = lambda b, t: (layer, 0, 0, 0)
    in_specs = [
        pl.BlockSpec(memory_space=pltpu.SMEM),
        pl.BlockSpec((1, tt, D_MODEL), lambda b, t: (b, t, 0)),
        pl.BlockSpec((1, D_MODEL), const2),
        pl.BlockSpec((None, D_MODEL, D_IN), layer3),
        pl.BlockSpec((None, D_MODEL, D_MODEL), layer3),
        pl.BlockSpec((CONV_W, GROUP_W), const2),
        pl.BlockSpec((None, S5_NG, S5_FOLD, S5_FOLD + S5_RI), layer4),
        pl.BlockSpec((None, S5_NG, S5_RI, S5_FOLD), layer4),
        pl.BlockSpec((3, S5_NG, S5_RI), const3),
        pl.BlockSpec((1, GROUP_W), const2),
        pl.BlockSpec((None, GROUP_W, 2 * GROUP_W), layer3),
        pl.BlockSpec((1, ATT_QH * ATT_HD), const2),
        pl.BlockSpec((1, ATT_KVH * ATT_HD), const2),
        pl.BlockSpec((1, GROUP_W), const2),
        pl.BlockSpec((1, GROUP_W), const2),
        pl.BlockSpec((1, D_MODEL), const2),
        pl.BlockSpec((tt, tt), const2),
    ]
    scratch = [
        pltpu.VMEM((N_CHUNKS, tt, LANES), F32),
        pltpu.VMEM((J_HG - J_HQ, tt * HG_PITCH // HG_BLOCK, LANES), F32),
        pltpu.VMEM((J_AQ - J_S5U, tt * HG_PITCH // HG_BLOCK, LANES), F32),
        pltpu.VMEM((8 + tt, GROUP_W), F32),
        pltpu.VMEM((WIN_CHUNKS * CHUNK + tt, LANES), BF16),
        pltpu.VMEM((WIN_CHUNKS * CHUNK + tt, LANES), BF16),
        pltpu.VMEM((HG_DK, GROUP_W), F32),
        pltpu.VMEM((tt, GROUP_W), F32),
        pltpu.VMEM((2, S5_NG, LANES), F32),
        pltpu.VMEM((nb * S5_PITCH, LANES), F32),
        pltpu.VMEM((nb * S5_PITCH, LANES), F32),
        pltpu.VMEM((S5_NG, nb, GROUP_W), F32),
        pltpu.VMEM((2, tt, LANES), F32),
        pltpu.VMEM((tt, GROUP_W), F32),
        pltpu.VMEM((tt, D_MODEL), BF16),
    ]
    return pl.pallas_call(
        functools.partial(_mixer_kernel, tt=tt),
        grid=(bsz, seq // tt),
        in_specs=in_specs,
        out_specs=pl.BlockSpec((1, tt, D_MODEL), lambda b, t: (b, t, 0)),
        out_shape=jax.ShapeDtypeStruct(x.shape, x.dtype),
        scratch_shapes=scratch,
        compiler_params=pltpu.CompilerParams(
            dimension_semantics=("parallel", "arbitrary"),
            vmem_limit_bytes=VMEM_LIMIT_BYTES),
        name="mixer_layer",
    )(sinks, x, nm, w_in, w_out, conv_w, w1, cexp, pq, dskip, wglu, qg, kg, lb, hgn, gn, perm)


def _ffn_kernel(x_ref, nf_ref, w1_ref, w2_ref, o_ref, acc_sc):
    x = x_ref[...]
    ms = jnp.mean(x * x, axis=-1, keepdims=True)
    hn = (x * lax.rsqrt(ms + EPS) * nf_ref[...]).astype(BF16)
    for j in range(D_FF // FFN_COLS):
        a = jnp.maximum(_dot(hn, w1_ref[:, j * FFN_COLS:(j + 1) * FFN_COLS]), 0.0)
        part = _dot((a * a).astype(BF16), w2_ref[j * FFN_COLS:(j + 1) * FFN_COLS, :])
        if j == 0:
            acc_sc[...] = x + part
        else:
            acc_sc[...] += part
    o_ref[...] = acc_sc[...]


def _ffn_layer(layer, x2, nf, w1, w2):
    rows = x2.shape[0]
    tm = min(FFN_TILE, rows)
    assert rows % tm == 0
    return pl.pallas_call(
        _ffn_kernel,
        grid=(rows // tm,),
        in_specs=[
            pl.BlockSpec((tm, D_MODEL), lambda i: (i, 0)),
            pl.BlockSpec((1, D_MODEL), lambda i: (0, 0)),
            pl.BlockSpec((None, D_MODEL, D_FF), lambda i: (layer, 0, 0), pipeline_mode=pl.Buffered(1)),
            pl.BlockSpec((None, D_FF, D_MODEL), lambda i: (layer, 0, 0), pipeline_mode=pl.Buffered(1)),
        ],
        out_specs=pl.BlockSpec((tm, D_MODEL), lambda i: (i, 0)),
        out_shape=jax.ShapeDtypeStruct(x2.shape, x2.dtype),
        scratch_shapes=[pltpu.VMEM((tm, D_MODEL), F32)],
        compiler_params=pltpu.CompilerParams(
            dimension_semantics=("parallel",),
            vmem_limit_bytes=VMEM_LIMIT_BYTES),
        name="ffn_layer",
    )(x2, nf, w1, w2)


def _s5_tables(lam_re, lam_im, b_re, b_im, c_re, c_im, log_dt):
    hp = lax.Precision.HIGHEST
    lr = jnp.minimum(lam_re.astype(F32), -1e-4)
    li = lam_im.astype(F32)
    dt = jnp.exp(log_dt.astype(F32))[:, None]
    mag = jnp.exp(lr * dt)
    ar = mag * jnp.cos(li * dt)
    ai = mag * jnp.sin(li * dt)
    den = lr * lr + li * li
    zr = ((ar - 1.0) * lr + ai * li) / den
    zi = (ai * lr - (ar - 1.0) * li) / den
    bre = b_re.astype(F32)
    bim = b_im.astype(F32)
    bbr = zr[..., None] * bre - zi[..., None] * bim
    bbi = zr[..., None] * bim + zi[..., None] * bre
    d = jnp.arange(S5_BLOCK + 1, dtype=F32)[:, None, None]
    magd = jnp.exp(d * (lr * dt))
    pr = magd * jnp.cos(d * (li * dt))
    pi = magd * jnp.sin(d * (li * dt))
    cre = c_re.astype(F32)[None]
    cim = c_im.astype(F32)[None]
    car = cre * pr[:, :, None, :] - cim * pi[:, :, None, :]
    cai = cre * pi[:, :, None, :] + cim * pr[:, :, None, :]
    c_a = jnp.concatenate([car, -cai], axis=3)[:S5_BLOCK].transpose(1, 0, 2, 3)
    kern = jnp.einsum('gmp,gpj->gmj', c_a.reshape(S5_NG, S5_FOLD, S5_RI),
                      jnp.concatenate([bbr, bbi], axis=1), precision=hp)
    kcat = jnp.swapaxes(kern, 1, 2)
    lane = jnp.arange(S5_FOLD)
    toep = jnp.stack([jnp.where(lane >= S5_GROUP * k, jnp.roll(kcat, S5_GROUP * k, axis=2), 0.0)
                      for k in range(S5_BLOCK)], axis=1)
    toep = toep.reshape(S5_NG, S5_FOLD, S5_FOLD)
    prk = jnp.flip(pr[:S5_BLOCK], axis=0).transpose(1, 0, 2)[:, :, None, :]
    pik = jnp.flip(pi[:S5_BLOCK], axis=0).transpose(1, 0, 2)[:, :, None, :]
    bbr_t = bbr.transpose(0, 2, 1)[:, None]
    bbi_t = bbi.transpose(0, 2, 1)[:, None]
    s_in = jnp.concatenate([prk * bbr_t - pik * bbi_t, prk * bbi_t + pik * bbr_t], axis=3)
    s_in = s_in.reshape(S5_NG, S5_FOLD, S5_RI)
    w1 = jnp.concatenate([toep, s_in], axis=2).astype(BF16)
    s_out = jnp.concatenate([car[1:], -cai[1:]], axis=3).transpose(1, 0, 2, 3)
    cexp = jnp.swapaxes(s_out.reshape(S5_NG, S5_FOLD, S5_RI), 1, 2).astype(BF16)
    p_rot = jnp.concatenate([pr[S5_BLOCK], pr[S5_BLOCK]], axis=-1)
    q_rot = jnp.concatenate([-pi[S5_BLOCK], pi[S5_BLOCK]], axis=-1)
    pq = jnp.stack([p_rot, q_rot, -q_rot])
    return w1, cexp, pq


def kernel(x, w_in, w_out, norm_mix, norm_ffn, conv_w, s5_lam_re, s5_lam_im, s5_b_re, s5_b_im, s5_c_re, s5_c_im, s5_d, s5_log_dt, s5_w_glu, attn_q_norm, attn_k_norm, attn_sinks, hg_lower_bounds, hg_out_norm, group_norm, w_ff1, w_ff2):
    bsz, seq, _ = x.shape
    depth = w_in.shape[0]
    lb_p = jax.nn.softmax(hg_lower_bounds.astype(F32), axis=0)
    lb_table = jnp.cumsum(lb_p, axis=0) - lb_p[0]
    def swap_mid_heads(a, base, axis):
        lo, mid, hi = base + ATT_HD, base + 2 * ATT_HD, base + 3 * ATT_HD
        h1 = lax.slice_in_dim(a, lo, mid, axis=axis)
        h2 = lax.slice_in_dim(a, mid, hi, axis=axis)
        a = lax.dynamic_update_slice_in_dim(a, h2, lo, axis=axis)
        return lax.dynamic_update_slice_in_dim(a, h1, mid, axis=axis)

    w1, cexp, pq = jax.vmap(_s5_tables)(s5_lam_re, s5_lam_im, s5_b_re, s5_b_im, s5_c_re, s5_c_im,
                                         s5_log_dt)
    w_in_b = swap_mid_heads(w_in.astype(BF16), J_AQ * LANES, 2)
    w_out_b = swap_mid_heads(w_out.astype(BF16), 2 * GROUP_W, 1)
    gn_b = swap_mid_heads(group_norm, 2 * GROUP_W, 1)
    w_glu_b = s5_w_glu.astype(BF16)
    w_ff1_b = w_ff1.astype(BF16)
    w_ff2_b = w_ff2.astype(BF16)
    qg = jnp.tile(attn_q_norm, (1, ATT_QH))
    kg = jnp.tile(attn_k_norm, (1, ATT_KVH))
    hgn = jnp.tile(hg_out_norm, (1, HG_HEADS))
    row = lambda a, l: a[l].reshape(1, -1)
    for l in range(depth):
        x = _mixer_layer(
            l, x, attn_sinks[l].astype(F32), row(norm_mix, l), w_in_b, w_out_b,
            conv_w[l].astype(F32), w1, cexp, pq[l], row(s5_d, l), w_glu_b,
            row(qg, l), row(kg, l), row(lb_table, l), row(hgn, l), row(gn_b, l))
        x = _ffn_layer(l, x.reshape(bsz * seq, D_MODEL), row(norm_ffn, l),
                       w_ff1_b, w_ff2_b).reshape(bsz, seq, D_MODEL)
    return x
```
